```python
import functools
import jax, jax.numpy as jnp
from jax import lax
import numpy as np

D_MODEL = 1024
BATCH = 8
SEQ = 2048
DEPTH = 4
DEC_BATCH = 32
DEC_SEQ = 1
PAST_LEN = 16384
PAGE_SIZE = 128

N_META = 16
A_HEADS = 4
A_DK = 128
A_DV = 128
A_QK = A_HEADS * A_DK
A_VW = A_HEADS * A_DV
CONV_W = 4
CHUNK = 64
B_HEADS = 8
B_NOPE = 64
B_ROPE = 32
B_DV = 64
Q_RANK = 384
KV_RANK = 256
ROPE_BASE = 10000.0
Q_BLOCK = 128
ATTN_SCALE = (B_NOPE + B_ROPE) ** -0.5
D_FF = 2816
DN_ALPHA = (2 * DEPTH) ** 0.25
DN_BETA = (8 * DEPTH) ** -0.25
LN_EPS = 1e-5
RMS_EPS = 1e-6
IN_SIZES = (Q_RANK, KV_RANK, B_ROPE, 2 * A_QK, A_VW, A_VW, A_HEADS, A_HEADS, D_MODEL, D_MODEL)
N_IN = sum(IN_SIZES)

kernel_name = "hybrid_mlstm_mla_macaron_deepnorm_step"

F32 = jnp.float32


def layer_norm(x, g, b):
    xf = x.astype(F32)
    mu = xf.mean(-1, keepdims=True)
    var = jnp.square(xf - mu).mean(-1, keepdims=True)
    return ((xf - mu) * lax.rsqrt(var + LN_EPS) * g + b).astype(x.dtype)


def rms_norm(x, g):
    xf = x.astype(F32)
    return (xf * lax.rsqrt(jnp.mean(xf * xf, -1, keepdims=True) + RMS_EPS) * g).astype(x.dtype)


def post_norm(x, sub, g, b):
    return layer_norm(DN_ALPHA * x + sub, g, b)


def swiglu(x, w_up, w_down):
    a, u = jnp.split(x @ w_up, 2, axis=-1)
    return (jax.nn.silu(a) * u) @ w_down


def split_in(z):
    return jnp.split(z, np.cumsum(IN_SIZES)[:-1].tolist(), axis=-1)


def rope_tables(pos):
    inv = ROPE_BASE ** (-jnp.arange(0, B_ROPE, 2, dtype=F32) / B_ROPE)
    ang = pos.astype(F32)[:, None] * inv[None, :]
    return jnp.cos(ang), jnp.sin(ang)


def apply_rope(x, cos, sin):
    shape = (cos.shape[0],) + (1,) * (x.ndim - 3) + (cos.shape[1],)
    c, s = cos.reshape(shape), sin.reshape(shape)
    xf = x.astype(F32)
    half = B_ROPE // 2
    x1, x2 = xf[..., :half], xf[..., half:]
    return jnp.concatenate([x1 * c - x2 * s, x1 * s + x2 * c], axis=-1).astype(x.dtype)


def causal_conv(u, prev, w, b):
    S = u.shape[1]
    up = jnp.concatenate([prev.astype(u.dtype), u], axis=1)
    out = b + w[0] * up[:, 0:S]
    for j in range(1, CONV_W):
        out = out + w[j] * up[:, j:j + S]
    return out, up[:, S:]


def mlstm_inputs(qk_c, v, ig, fg):
    Bsz, S, _ = v.shape
    a = jax.nn.silu(qk_c.astype(F32))

    def heads(t, d):
        return t.reshape(Bsz, S, A_HEADS, d).transpose(0, 2, 1, 3).astype(F32)

    q = heads(a[..., :A_QK], A_DK)
    k = heads(a[..., A_QK:], A_DK) * (A_DK ** -0.5)
    vv = heads(v, A_DV)
    igh = ig.astype(F32).transpose(0, 2, 1)
    lf = jax.nn.log_sigmoid(fg.astype(F32)).transpose(0, 2, 1)
    return q, k, vv, igh, lf


def mlstm_chunk(carry, blk):
    C, n, m = carry
    q, k, v, ig, lf = blk
    L = q.shape[2]
    b = jnp.cumsum(lf, axis=-1)
    causal = jnp.tril(jnp.ones((L, L), dtype=bool))
    dmat = jnp.where(causal, b[..., :, None] - b[..., None, :] + ig[..., None, :], -jnp.inf)
    inter = b + m[..., None]
    m_row = jnp.maximum(inter, dmat.max(-1))
    s = jnp.einsum('bhtd,bhsd->bhts', q, k) * jnp.exp(dmat - m_row[..., None])
    w_prev = jnp.exp(inter - m_row)
    num = w_prev[..., None] * jnp.einsum('bhtd,bhde->bhte', q, C) + jnp.einsum('bhts,bhse->bhte', s, v)
    den = w_prev * jnp.einsum('bhtd,bhd->bht', q, n) + s.sum(-1)
    h = num / jnp.maximum(jnp.abs(den), jnp.exp(-m_row))[..., None]
    b_last = b[..., -1]
    g = b_last[..., None] - b + ig
    m_new = jnp.maximum(b_last + m, g.max(-1))
    decay = jnp.exp(b_last + m - m_new)
    wk = jnp.exp(g - m_new[..., None])[..., None] * k
    C_new = decay[..., None, None] * C + jnp.einsum('bhsd,bhse->bhde', wk, v)
    n_new = decay[..., None] * n + wk.sum(2)
    return (C_new, n_new, m_new), h


def mlstm_out(h, o, p):
    Bsz, H, S, DV = h.shape
    h = jnp.transpose(h, (0, 2, 1, 3))
    mu = h.mean(-1, keepdims=True)
    var = jnp.square(h - mu).mean(-1, keepdims=True)
    h = (h - mu) * lax.rsqrt(var + LN_EPS) * p['mh_g'].reshape(A_HEADS, A_DV)
    h = h.reshape(Bsz, S, A_VW) * jax.nn.sigmoid(o.astype(F32))
    return h.astype(o.dtype) @ p['w_pa']


def mla_prep(cq, ckv, kr, pos, p):
    cos, sin = rope_tables(pos)
    q = jnp.einsum('bsc,chd->bshd', rms_norm(cq, p['q_norm_g']), p['w_uq'])
    qn, qr = q[..., :B_NOPE], apply_rope(q[..., B_NOPE:], cos, sin)
    c_kv = rms_norm(ckv, p['kv_norm_g'])
    k_r = apply_rope(kr, cos, sin)
    return qn, qr, c_kv, k_r


def attend(qn, qr, qpos, kn, kr, v, kpos):
    s = (jnp.einsum('bqhd,bkhd->bhqk', qn, kn) + jnp.einsum('bqhe,bke->bhqk', qr, kr)).astype(F32) * ATTN_SCALE
    s = jnp.where(kpos[None, None, None, :] <= qpos[None, None, :, None], s, -jnp.inf)
    w = jax.nn.softmax(s, axis=-1).astype(v.dtype)
    return jnp.einsum('bhqk,bkhd->bqhd', w, v)


def mla_prompt_attend(qn, qr, c_kv, k_r, pos, p):
    Bsz = qn.shape[0]
    kn = jnp.einsum('bsr,rhd->bshd', c_kv, p['w_uk'])
    vb = jnp.einsum('bsr,rhd->bshd', c_kv, p['w_uv'])
    o_meta = attend(qn[:, :N_META], qr[:, :N_META], pos[:N_META],
                    kn[:, :N_META], k_r[:, :N_META], vb[:, :N_META], pos[:N_META])

    def blocks(t):
        t = t[:, N_META:]
        nb = t.shape[1] // Q_BLOCK
        return jnp.moveaxis(t.reshape((t.shape[0], nb, Q_BLOCK) + t.shape[2:]), 1, 0)

    qpos_b = pos[N_META:].reshape(-1, Q_BLOCK)
    o_blk = lax.map(lambda a: attend(a[0], a[1], a[2], kn, k_r, vb, pos),
                    (blocks(qn), blocks(qr), qpos_b))
    o_rest = jnp.moveaxis(o_blk, 0, 1).reshape(Bsz, -1, B_HEADS, B_DV)
    return jnp.concatenate([o_meta, o_rest], axis=1)


def mla_sample_attend(qn, qr, c_kv, k_r, p, ckv_pool, kr_pool, page_table):
    Bsz, T = qn.shape[0], qn.shape[1]
    ckv_past = ckv_pool[page_table].reshape(Bsz, -1, KV_RANK).astype(c_kv.dtype)
    kr_past = kr_pool[page_table].reshape(Bsz, -1, B_ROPE).astype(k_r.dtype)
    q_lat = jnp.einsum('bthd,rhd->bthr', qn, p['w_uk'])
    s_past = (jnp.einsum('bthr,bsr->bhts', q_lat, ckv_past)
              + jnp.einsum('bthe,bse->bhts', qr, kr_past)).astype(F32) * ATTN_SCALE
    s_new = (jnp.einsum('bthr,bsr->bhts', q_lat, c_kv)
             + jnp.einsum('bthe,bse->bhts', qr, k_r)).astype(F32) * ATTN_SCALE
    s_new = jnp.where(jnp.tril(jnp.ones((T, T), dtype=bool)), s_new, -jnp.inf)
    w = jax.nn.softmax(jnp.concatenate([s_past, s_new], axis=-1), axis=-1).astype(c_kv.dtype)
    n_past = ckv_past.shape[1]
    o_lat = (jnp.einsum('bhts,bsr->bthr', w[..., :n_past], ckv_past)
             + jnp.einsum('bhts,bsr->bthr', w[..., n_past:], c_kv))
    return jnp.einsum('bthr,rhd->bthd', o_lat, p['w_uv'])


def merge(y_a, y_b, ga, gb, p):
    return (jax.nn.sigmoid(ga) * y_a + jax.nn.sigmoid(gb) * y_b) @ p['w_o']


def prompt_mixer(parts, p, pos):
    cq, ckv, kr, qk, v, o, ig, fg, ga, gb = parts
    Bsz = v.shape[0]
    qk_c, conv_state = causal_conv(qk, jnp.zeros((Bsz, CONV_W - 1, 2 * A_QK), qk.dtype), p['conv_w'], p['conv_b'])
    q, k, vv, igh, lf = mlstm_inputs(qk_c, v, ig, fg)
    state0 = (jnp.zeros((Bsz, A_HEADS, A_DK, A_DV), F32), jnp.zeros((Bsz, A_HEADS, A_DK), F32),
              jnp.zeros((Bsz, A_HEADS), F32))
    carry, h_meta = mlstm_chunk(state0, (q[:, :, :N_META], k[:, :, :N_META], vv[:, :, :N_META],
                                         igh[:, :, :N_META], lf[:, :, :N_META]))

    def to_chunks(t):
        t = t[:, :, N_META:]
        nc = t.shape[2] // CHUNK
        return jnp.moveaxis(t.reshape(t.shape[:2] + (nc, CHUNK) + t.shape[3:]), 2, 0)

    carry, h_rest = lax.scan(mlstm_chunk, carry, (to_chunks(q), to_chunks(k), to_chunks(vv),
                                                  to_chunks(igh), to_chunks(lf)))
    h_rest = jnp.moveaxis(h_rest, 0, 2).reshape(Bsz, A_HEADS, -1, A_DV)
    y_a = mlstm_out(jnp.concatenate([h_meta, h_rest], axis=2), o, p)
    qn, qr, c_kv, k_r = mla_prep(cq, ckv, kr, pos, p)
    o_b = mla_prompt_attend(qn, qr, c_kv, k_r, pos, p)
    y_b = o_b.reshape(Bsz, -1, B_HEADS * B_DV) @ p['w_pb']
    C_f, n_f, m_f = carry
    return merge(y_a, y_b, ga, gb, p), (c_kv, k_r, C_f, n_f, m_f, conv_state)


def sample_mixer(parts, p, pos, ckv_pool, kr_pool, page_table, C, n, m, conv_prev):
    cq, ckv, kr, qk, v, o, ig, fg, ga, gb = parts
    Bsz = v.shape[0]
    qk_c, conv_state = causal_conv(qk, conv_prev, p['conv_w'], p['conv_b'])
    q, k, vv, igh, lf = mlstm_inputs(qk_c, v, ig, fg)
    (C_f, n_f, m_f), h = mlstm_chunk((C.astype(F32), n.astype(F32), m.astype(F32)), (q, k, vv, igh, lf))
    y_a = mlstm_out(h, o, p)
    qn, qr, c_kv, k_r = mla_prep(cq, ckv, kr, pos, p)
    o_b = mla_sample_attend(qn, qr, c_kv, k_r, p, ckv_pool, kr_pool, page_table)
    y_b = o_b.reshape(Bsz, -1, B_HEADS * B_DV) @ p['w_pb']
    return merge(y_a, y_b, ga, gb, p), (c_kv, k_r, C_f, n_f, m_f, conv_state)


def layer(x, p, mixer):
    x = post_norm(x, 0.5 * swiglu(x, p['ffn1_up'], p['ffn1_down']), p['ln_g'][0], p['ln_b'][0])
    parts = split_in(x @ p['w_in'] + p['b_in'])
    y, state = mixer(parts, p)
    x = post_norm(x, y, p['ln_g'][1], p['ln_b'][1])
    x = post_norm(x, 0.5 * swiglu(x, p['ffn2_up'], p['ffn2_down']), p['ln_g'][2], p['ln_b'][2])
    return x, state


def setup_inputs(seed: int = 0) -> dict:
    key = jax.random.key(seed)
    ks = iter(jax.random.split(key, 40))

    def nrm(shape, scale):
        return scale * jax.random.normal(next(ks), shape, F32)

    n_pages = PAST_LEN // PAGE_SIZE
    used = DEC_BATCH * n_pages
    n_pool = (5 * used + 3) // 4
    page_table = jax.random.permutation(next(ks), n_pool)[:used].reshape(DEC_BATCH, n_pages).astype(jnp.int32)
    fg_off = sum(IN_SIZES[:7])
    b_in = nrm((DEPTH, N_IN), 0.02).at[:, fg_off:fg_off + A_HEADS].add(jnp.linspace(3.0, 6.0, A_HEADS))
    return {
        'x_prompt': nrm((BATCH, SEQ, D_MODEL), 1.0),
        'x_sample': nrm((DEC_BATCH, DEC_SEQ, D_MODEL), 1.0),
        'cache_ckv': nrm((DEPTH, n_pool, PAGE_SIZE, KV_RANK), 1.0),
        'cache_krope': nrm((DEPTH, n_pool, PAGE_SIZE, B_ROPE), 1.0),
        'page_table': page_table,
        'state_C': nrm((DEPTH, DEC_BATCH, A_HEADS, A_DK, A_DV), 1.0),
        'state_n': nrm((DEPTH, DEC_BATCH, A_HEADS, A_DK), 1.0),
        'state_m': nrm((DEPTH, DEC_BATCH, A_HEADS), 1.0),
        'state_conv': nrm((DEPTH, DEC_BATCH, CONV_W - 1, 2 * A_QK), 1.0),
        'meta': nrm((N_META, D_MODEL), 1.0),
        'w_in': nrm((DEPTH, D_MODEL, N_IN), D_MODEL ** -0.5),
        'b_in': b_in,
        'conv_w': nrm((DEPTH, CONV_W, 2 * A_QK), CONV_W ** -0.5),
        'conv_b': nrm((DEPTH, 2 * A_QK), 0.02),
        'mh_g': 1.0 + nrm((DEPTH, A_VW), 0.02),
        'q_norm_g': 1.0 + nrm((DEPTH, Q_RANK), 0.02),
        'kv_norm_g': 1.0 + nrm((DEPTH, KV_RANK), 0.02),
        'w_uq': nrm((DEPTH, Q_RANK, B_HEADS, B_NOPE + B_ROPE), Q_RANK ** -0.5),
        'w_uk': nrm((DEPTH, KV_RANK, B_HEADS, B_NOPE), KV_RANK ** -0.5),
        'w_uv': nrm((DEPTH, KV_RANK, B_HEADS, B_DV), KV_RANK ** -0.5),
        'w_pa': nrm((DEPTH, A_VW, D_MODEL), A_VW ** -0.5),
        'w_pb': nrm((DEPTH, B_HEADS * B_DV, D_MODEL), (B_HEADS * B_DV) ** -0.5),
        'w_o': nrm((DEPTH, D_MODEL, D_MODEL), DN_BETA * D_MODEL ** -0.5),
        'ffn1_up': nrm((DEPTH, D_MODEL, 2 * D_FF), D_MODEL ** -0.5),
        'ffn1_down': nrm((DEPTH, D_FF, D_MODEL), DN_BETA * D_FF ** -0.5),
        'ffn2_up': nrm((DEPTH, D_MODEL, 2 * D_FF), D_MODEL ** -0.5),
        'ffn2_down': nrm((DEPTH, D_FF, D_MODEL), DN_BETA * D_FF ** -0.5),
        'ln_g': 1.0 + nrm((DEPTH, 3, D_MODEL), 0.02),
        'ln_b': nrm((DEPTH, 3, D_MODEL), 0.02),
    }


def reference(x_prompt, x_sample, cache_ckv, cache_krope, page_table, state_C, state_n, state_m, state_conv,
              meta, w_in, b_in, conv_w, conv_b, mh_g, q_norm_g, kv_norm_g, w_uq, w_uk, w_uv, w_pa, w_pb, w_o,
              ffn1_up, ffn1_down, ffn2_up, ffn2_down, ln_g, ln_b):
    Bsz = x_prompt.shape[0]
    xp = jnp.concatenate([jnp.broadcast_to(meta[None].astype(x_prompt.dtype), (Bsz, N_META, D_MODEL)), x_prompt], axis=1)
    xs = x_sample
    pos_p = jnp.arange(xp.shape[1], dtype=jnp.int32)
    pos_s = PAST_LEN + jnp.arange(xs.shape[1], dtype=jnp.int32)
    st_p, st_s = [], []
    for l in range(DEPTH):
        p = {'w_in': w_in[l], 'b_in': b_in[l], 'conv_w': conv_w[l], 'conv_b': conv_b[l], 'mh_g': mh_g[l],
             'q_norm_g': q_norm_g[l], 'kv_norm_g': kv_norm_g[l], 'w_uq': w_uq[l], 'w_uk': w_uk[l],
             'w_uv': w_uv[l], 'w_pa': w_pa[l], 'w_pb': w_pb[l], 'w_o': w_o[l],
             'ffn1_up': ffn1_up[l], 'ffn1_down': ffn1_down[l], 'ffn2_up': ffn2_up[l], 'ffn2_down': ffn2_down[l],
             'ln_g': ln_g[l], 'ln_b': ln_b[l]}
        xp, sp = layer(xp, p, functools.partial(prompt_mixer, pos=pos_p))
        xs, ss = layer(xs, p, functools.partial(sample_mixer, pos=pos_s, ckv_pool=cache_ckv[l],
                                                kr_pool=cache_krope[l], page_table=page_table,
                                                C=state_C[l], n=state_n[l], m=state_m[l],
                                                conv_prev=state_conv[l]))
        st_p.append(sp)
        st_s.append(ss)
    y_prompt = xp[:, N_META:]
    y_sample = xs
    new_ckv_prompt = jnp.stack([s[0] for s in st_p])
    new_krope_prompt = jnp.stack([s[1] for s in st_p])
    new_C_prompt = jnp.stack([s[2] for s in st_p])
    new_n_prompt = jnp.stack([s[3] for s in st_p])
    new_m_prompt = jnp.stack([s[4] for s in st_p])
    new_conv_prompt = jnp.stack([s[5] for s in st_p])
    new_ckv_sample = jnp.stack([s[0] for s in st_s])
    new_krope_sample = jnp.stack([s[1] for s in st_s])
    new_C_sample = jnp.stack([s[2] for s in st_s])
    new_n_sample = jnp.stack([s[3] for s in st_s])
    new_m_sample = jnp.stack([s[4] for s in st_s])
    new_conv_sample = jnp.stack([s[5] for s in st_s])
    return (y_prompt, y_sample, new_ckv_prompt, new_krope_prompt, new_C_prompt, new_n_prompt, new_m_prompt,
            new_conv_prompt, new_ckv_sample, new_krope_sample, new_C_sample, new_n_sample, new_m_sample,
            new_conv_sample)
```

```python
import functools

import numpy as np
import jax
import jax.numpy as jnp
from jax import lax
from jax.experimental import pallas as pl
from jax.experimental.pallas import tpu as pltpu

F32 = jnp.float32
BF16 = jnp.bfloat16

D_MODEL = 1024
BATCH = 8
SEQ = 2048
DEPTH = 4
DEC_BATCH = 32
PAST_LEN = 16384
PAGE_SIZE = 128
N_PAGES = PAST_LEN // PAGE_SIZE
N_META = 16
A_HEADS = 4
A_DK = 128
A_DV = 128
A_QK = A_HEADS * A_DK
A_VW = A_HEADS * A_DV
CONV_W = 4
B_HEADS = 8
B_NOPE = 64
B_ROPE = 32
B_DV = 64
Q_RANK = 384
KV_RANK = 256
ROPE_BASE = 10000.0
ATTN_SCALE = (B_NOPE + B_ROPE) ** -0.5
D_FF = 2816
DN_ALPHA = (2 * DEPTH) ** 0.25
LN_EPS = 1e-5
RMS_EPS = 1e-6
IN_SIZES = (Q_RANK, KV_RANK, B_ROPE, 2 * A_QK, A_VW, A_VW, A_HEADS, A_HEADS, D_MODEL, D_MODEL)

LANE = 128
HALF_ROPE = B_ROPE // 2
ROPE_LANE = B_NOPE

TM = 512
N_REAL = BATCH * SEQ
META_OFF = N_REAL
SAMP_OFF = META_OFF + BATCH * N_META
TAIL_ROWS = BATCH * N_META + DEC_BATCH
T_PAD = N_REAL + TM
N_TILES = T_PAD // TM
REAL_TILES_PER_SEQ = SEQ // TM
S_ALL = SEQ + N_META

G_CQ, G_CKV, G_SG, G_QK, G_V, G_O, G_GA, G_GB = 0, 384, 640, 768, 1792, 2304, 2816, 3840
N_IN_P = 4864

FF_CHUNKS = 2
FF_C = D_FF // FF_CHUNKS

MLSTM_L = 256
CONV_PAD = 8

ATT_TQ = 512
ATT_TK = 512
PAGES_PER_STEP = 8

VMEM_LIMIT = 56 * 1024 * 1024


def _cparams(sem):
    return pltpu.CompilerParams(dimension_semantics=sem, vmem_limit_bytes=VMEM_LIMIT)


def _const_spec(shape):
    nd = len(shape)
    return pl.BlockSpec(shape, lambda *_: (0,) * nd, pipeline_mode=pl.Buffered(1))


def _layer_norm(z, g, b):
    mu = jnp.mean(z, axis=-1, keepdims=True)
    zc = z - mu
    var = jnp.mean(zc * zc, axis=-1, keepdims=True)
    return zc * lax.rsqrt(var + LN_EPS) * g + b


def _rms_norm(z, g):
    return z * lax.rsqrt(jnp.mean(z * z, axis=-1, keepdims=True) + RMS_EPS) * g


def _dot(a, b):
    return jnp.dot(a, b, preferred_element_type=F32)


def _dot_nt(a, b):
    return lax.dot_general(a, b, (((1,), (1,)), ((), ())), preferred_element_type=F32)


def _dot_tn(a, b):
    return lax.dot_general(a, b, (((0,), (0,)), ((), ())), preferred_element_type=F32)


def _log_sigmoid(x):
    return jnp.minimum(x, 0.0) - jnp.log1p(jnp.exp(-jnp.abs(x)))


def _ffn_kernel(x_ref, wup_ref, wdn_ref, g_ref, b_ref, o_ref):
    x = x_ref[...]
    xb = x.astype(BF16)
    y = jnp.zeros_like(x)
    for c in range(FF_CHUNKS):
        lo = c * FF_C
        a = _dot(xb, wup_ref[:, lo:lo + FF_C])
        u = _dot(xb, wup_ref[:, D_FF + lo:D_FF + lo + FF_C])
        h = (a * jax.nn.sigmoid(a) * u).astype(BF16)
        y = y + _dot(h, wdn_ref[lo:lo + FF_C, :])
    o_ref[...] = _layer_norm(DN_ALPHA * x + 0.5 * y, g_ref[...], b_ref[...])


def _ffn(x, w_up, w_down, g, b):
    row = pl.BlockSpec((TM, D_MODEL), lambda i: (i, 0))
    return pl.pallas_call(
        _ffn_kernel,
        grid=(N_TILES,),
        in_specs=[row, _const_spec((D_MODEL, 2 * D_FF)), _const_spec((D_FF, D_MODEL)),
                  _const_spec((1, D_MODEL)), _const_spec((1, D_MODEL))],
        out_specs=row,
        out_shape=jax.ShapeDtypeStruct((T_PAD, D_MODEL), F32),
        compiler_params=_cparams(("parallel",)),
        name="ffn_postnorm",
    )(x, w_up, w_down, g, b)


def _rope(z, tc, ts1, ts2):
    return z * tc + pltpu.roll(z, LANE - HALF_ROPE, 1) * ts1 + pltpu.roll(z, HALF_ROPE, 1) * ts2


def _inproj_kernel(x_ref, w_ref, b_ref, qg_ref, kvg_ref, wuq_ref, wkf_ref, wuv_ref, tc_ref, ts1_ref, ts2_ref,
                   q_ref, k_ref, vb_ref, ckv_ref, sg_ref, qk_ref, vm_ref, om_ref, ga_ref, gb_ref):
    xb = x_ref[...].astype(BF16)

    def proj(off, n):
        return _dot(xb, w_ref[:, off:off + n]) + b_ref[:, off:off + n]

    tc, ts1, ts2 = tc_ref[...], ts1_ref[...], ts2_ref[...]
    cqn = _rms_norm(proj(G_CQ, Q_RANK), qg_ref[...]).astype(BF16)
    q = _dot(cqn, wuq_ref[...])
    for h in range(B_HEADS):
        q_ref[:, h * LANE:(h + 1) * LANE] = _rope(q[:, h * LANE:(h + 1) * LANE], tc, ts1, ts2).astype(BF16)
    ckvn = _rms_norm(proj(G_CKV, KV_RANK), kvg_ref[...])
    ckv_ref[...] = ckvn
    sg = _rope(proj(G_SG, LANE), tc, ts1, ts2)
    sg_ref[...] = sg
    kin = jnp.concatenate([ckvn.astype(BF16), sg.astype(BF16)], axis=1)
    k_ref[...] = _dot(kin, wkf_ref[...]).astype(BF16)
    vb_ref[...] = _dot(kin[:, :KV_RANK], wuv_ref[...]).astype(BF16)
    qk_ref[...] = proj(G_QK, 2 * A_QK)
    vm_ref[...] = proj(G_V, A_VW).astype(BF16)
    om_ref[...] = proj(G_O, A_VW)
    ga_ref[...] = proj(G_GA, D_MODEL)
    gb_ref[...] = proj(G_GB, D_MODEL)


def _inproj(x, wp, tabs):
    def row(n):
        return pl.BlockSpec((TM, n), lambda i: (i, 0))

    tab = pl.BlockSpec((TM, LANE), lambda i: (jnp.where(i < N_REAL // TM, i % REAL_TILES_PER_SEQ,
                                                       REAL_TILES_PER_SEQ), 0))
    widths = (B_HEADS * LANE, B_HEADS * LANE, B_HEADS * B_DV, KV_RANK, LANE, 2 * A_QK, A_VW, A_VW, D_MODEL, D_MODEL)
    dtypes = (BF16, BF16, BF16, F32, F32, F32, BF16, F32, F32, F32)
    return pl.pallas_call(
        _inproj_kernel,
        grid=(N_TILES,),
        in_specs=[row(D_MODEL), _const_spec((D_MODEL, N_IN_P)), _const_spec((1, N_IN_P)),
                  _const_spec((1, Q_RANK)), _const_spec((1, KV_RANK)),
                  _const_spec((Q_RANK, B_HEADS * LANE)), _const_spec((KV_RANK + LANE, B_HEADS * LANE)),
                  _const_spec((KV_RANK, B_HEADS * B_DV)), tab, tab, tab],
        out_specs=[row(n) for n in widths],
        out_shape=[jax.ShapeDtypeStruct((T_PAD, n), d) for n, d in zip(widths, dtypes)],
        compiler_params=_cparams(("parallel",)),
        name="in_proj",
    )(x, wp["w_in"], wp["b_in"], wp["qg"], wp["kvg"], wp["w_uq"], wp["w_kf"], wp["w_uv"], *tabs)


def _mlstm_prompt_kernel(qm_ref, km_ref, q0_ref, k0_ref, vm_ref, v0_ref, om_ref, o0_ref, gr_ref, gc_ref,
                         cwq_ref, cwk_ref, cbq_ref, cbk_ref, mhg_ref,
                         hm_ref, h0_ref, cx_ref, m_ref, uq_ref, uk_ref):
    for u_ref, a0_ref, am_ref in ((uq_ref, q0_ref, qm_ref), (uk_ref, k0_ref, km_ref)):
        u_ref[0:CONV_PAD, :] = jnp.zeros((CONV_PAD, LANE), F32)
        u_ref[CONV_PAD:CONV_PAD + N_META, :] = a0_ref[...]
        u_ref[CONV_PAD + N_META:, :] = am_ref[...]

    def conv_silu(u_ref, w_ref, b_ref, t0, length):
        acc = b_ref[...]
        for j in range(CONV_W):
            lo = CONV_PAD - (CONV_W - 1) + j + t0
            acc = acc + w_ref[j:j + 1, :] * u_ref[lo:lo + length, :]
        return acc * jax.nn.sigmoid(acc)

    def chunk(t0, length, v_ref, o_ref, h_ref, r0, cx, m):
        q = conv_silu(uq_ref, cwq_ref, cbq_ref, t0, length)
        k = conv_silu(uk_ref, cwk_ref, cbk_ref, t0, length) * (A_DK ** -0.5)
        qb, kb = q.astype(BF16), k.astype(BF16)
        ig_r = gr_ref[0:1, t0:t0 + length]
        lf_r = _log_sigmoid(gr_ref[1:2, t0:t0 + length])
        ig_c = gc_ref[t0:t0 + length, 0:1]
        lf_c = _log_sigmoid(gc_ref[t0:t0 + length, 1:2])
        row = lax.broadcasted_iota(jnp.int32, (length, length), 0)
        col = lax.broadcasted_iota(jnp.int32, (length, length), 1)
        tri = row >= col
        b_c = jnp.sum(jnp.where(tri, lf_r, 0.0), axis=1, keepdims=True)
        b_r = jnp.sum(jnp.where(row <= col, lf_c, 0.0), axis=0, keepdims=True)
        dmat = jnp.where(tri, b_c - b_r + ig_r, -jnp.inf)
        inter = b_c + m
        m_row = jnp.maximum(inter, jnp.max(dmat, axis=1, keepdims=True))
        s = _dot_nt(qb, kb) * jnp.exp(dmat - m_row)
        w_prev = jnp.exp(inter - m_row)
        lane = lax.broadcasted_iota(jnp.int32, (length, LANE), 1)
        ve = jnp.concatenate([v_ref[r0:r0 + length, :], (lane == 0).astype(BF16)], axis=1)
        tot = w_prev * _dot(qb, cx.astype(BF16)) + _dot(s.astype(BF16), ve)
        h = tot[:, :A_DV] / jnp.maximum(jnp.abs(tot[:, A_DV:A_DV + 1]), jnp.exp(-m_row))
        b_last = b_c[length - 1:length, :]
        g = b_last - b_c + ig_c
        m_new = jnp.maximum(b_last + m, jnp.max(g, axis=0, keepdims=True))
        decay = jnp.exp(b_last + m - m_new)
        wk = (jnp.exp(g - m_new) * k).astype(BF16)
        cx_new = decay * cx + _dot_tn(wk, ve)
        mu = jnp.mean(h, axis=-1, keepdims=True)
        hc = h - mu
        var = jnp.mean(hc * hc, axis=-1, keepdims=True)
        hn = hc * lax.rsqrt(var + LN_EPS) * mhg_ref[...] * jax.nn.sigmoid(o_ref[r0:r0 + length, :])
        h_ref[r0:r0 + length, :] = hn.astype(BF16)
        return cx_new, m_new

    cx = jnp.zeros((A_DK, 2 * LANE), F32)
    m = jnp.zeros((1, 1), F32)
    cx, m = chunk(0, N_META, v0_ref, o0_ref, h0_ref, 0, cx, m)
    for c in range(SEQ // MLSTM_L):
        cx, m = chunk(N_META + c * MLSTM_L, MLSTM_L, vm_ref, om_ref, hm_ref, c * MLSTM_L, cx, m)
    cx_ref[...] = cx
    m_ref[...] = jnp.broadcast_to(m, (1, LANE))


def _mlstm_prompt(qk, vm, om, gr, gc, conv_w, conv_b, mh_g):
    meta_blk = META_OFF // N_META

    def main(off):
        return pl.BlockSpec((SEQ, LANE), lambda b, h: (b, h + off))

    def meta(off):
        return pl.BlockSpec((N_META, LANE), lambda b, h: (meta_blk + b, h + off))

    def wcol(rows, off):
        return pl.BlockSpec((rows, LANE), lambda b, h: (0, h + off))

    return pl.pallas_call(
        _mlstm_prompt_kernel,
        grid=(BATCH, A_HEADS),
        in_specs=[main(0), main(A_HEADS), meta(0), meta(A_HEADS), main(0), meta(0), main(0), meta(0),
                  pl.BlockSpec((None, None, 2, S_ALL), lambda b, h: (b, h, 0, 0)),
                  pl.BlockSpec((None, None, S_ALL, 2), lambda b, h: (b, h, 0, 0)),
                  wcol(CONV_W, 0), wcol(CONV_W, A_HEADS), wcol(1, 0), wcol(1, A_HEADS), wcol(1, 0)],
        out_specs=[main(0),
                   pl.BlockSpec((N_META, LANE), lambda b, h: (b, h)),
                   pl.BlockSpec((None, None, A_DK, 2 * LANE), lambda b, h: (b, h, 0, 0)),
                   pl.BlockSpec((None, None, 1, LANE), lambda b, h: (b, h, 0, 0))],
        out_shape=[jax.ShapeDtypeStruct((T_PAD, A_VW), BF16),
                   jax.ShapeDtypeStruct((BATCH * N_META, A_VW), BF16),
                   jax.ShapeDtypeStruct((BATCH, A_HEADS, A_DK, 2 * LANE), F32),
                   jax.ShapeDtypeStruct((BATCH, A_HEADS, 1, LANE), F32)],
        scratch_shapes=[pltpu.VMEM((CONV_PAD + S_ALL, LANE), F32), pltpu.VMEM((CONV_PAD + S_ALL, LANE), F32)],
        compiler_params=_cparams(("parallel", "parallel")),
        name="mlstm_prompt",
    )(qk, qk, qk, qk, vm, vm, om, om, gr, gc, conv_w, conv_w, conv_b, conv_b, mh_g)


def _mlstm_sample_kernel(qk_ref, cprev_ref, v_ref, o_ref, sg_ref, m_ref, c_ref, n_ref, cw_ref, cb_ref, mhg_ref,
                         hn_ref, cnew_ref, nnew_ref, mnew_ref):
    acc = (cb_ref[...] + jnp.sum(cw_ref[0:CONV_W - 1, :] * cprev_ref[...], axis=0, keepdims=True)
           + cw_ref[CONV_W - 1:CONV_W, :] * qk_ref[...])
    a = acc * jax.nn.sigmoid(acc)
    sg = sg_ref[...]
    m_in = m_ref[...]
    row8 = lax.broadcasted_iota(jnp.int32, (8, LANE), 0)
    for h in range(A_HEADS):
        q = a[:, h * A_DK:(h + 1) * A_DK]
        k = a[:, A_QK + h * A_DK:A_QK + (h + 1) * A_DK] * (A_DK ** -0.5)
        v = v_ref[:, h * A_DV:(h + 1) * A_DV]
        ig = sg[:, h:h + 1]
        lf = _log_sigmoid(sg[:, A_HEADS + h:A_HEADS + h + 1])
        m = m_in[:, h:h + 1]
        c = c_ref[h]
        n = n_ref[h:h + 1, :]
        inter = lf + m
        m_row = jnp.maximum(inter, ig)
        s = jnp.sum(q * k, axis=-1, keepdims=True) * jnp.exp(ig - m_row)
        w_prev = jnp.exp(inter - m_row)
        q8 = jnp.broadcast_to(q, (8, A_DK)).astype(BF16)
        qc = _dot(q8, c.astype(BF16))[0:1, :]
        num = w_prev * qc + s * v
        den = w_prev * jnp.sum(q * n, axis=-1, keepdims=True) + s
        hh = num / jnp.maximum(jnp.abs(den), jnp.exp(-m_row))
        wk = jnp.exp(ig - m_row) * k
        wk8 = jnp.where(row8 == 0, jnp.broadcast_to(wk, (8, A_DK)), 0.0).astype(BF16)
        v8 = jnp.broadcast_to(v, (8, A_DV)).astype(BF16)
        cnew_ref[h] = w_prev * c + _dot_tn(wk8, v8)
        nnew_ref[h:h + 1, :] = w_prev * n + wk
        mnew_ref[:, h:h + 1] = m_row
        mu = jnp.mean(hh, axis=-1, keepdims=True)
        hc = hh - mu
        var = jnp.mean(hc * hc, axis=-1, keepdims=True)
        hn = hc * lax.rsqrt(var + LN_EPS) * mhg_ref[:, h * A_DV:(h + 1) * A_DV]
        hn_ref[:, h * A_DV:(h + 1) * A_DV] = hn * jax.nn.sigmoid(o_ref[:, h * A_DV:(h + 1) * A_DV])


def _mlstm_sample(qk_s, conv_prev, v_s, o_s, sg_s, m_s, c_s, n_s, conv_w, conv_b, mh_g):
    def per_b(*shape):
        nd = len(shape)
        return pl.BlockSpec((None,) + shape, lambda b: (b,) + (0,) * nd)

    return pl.pallas_call(
        _mlstm_sample_kernel,
        grid=(DEC_BATCH,),
        in_specs=[per_b(1, 2 * A_QK), per_b(CONV_W - 1, 2 * A_QK), per_b(1, A_VW), per_b(1, A_VW), per_b(1, LANE),
                  per_b(1, A_HEADS), per_b(A_HEADS, A_DK, A_DV), per_b(A_HEADS, A_DK),
                  _const_spec((CONV_W, 2 * A_QK)), _const_spec((1, 2 * A_QK)), _const_spec((1, A_VW))],
        out_specs=[per_b(1, A_VW), per_b(A_HEADS, A_DK, A_DV), per_b(A_HEADS, A_DK), per_b(1, A_HEADS)],
        out_shape=[jax.ShapeDtypeStruct((DEC_BATCH, 1, A_VW), F32),
                   jax.ShapeDtypeStruct((DEC_BATCH, A_HEADS, A_DK, A_DV), F32),
                   jax.ShapeDtypeStruct((DEC_BATCH, A_HEADS, A_DK), F32),
                   jax.ShapeDtypeStruct((DEC_BATCH, 1, A_HEADS), F32)],
        compiler_params=_cparams(("parallel",)),
        name="mlstm_sample",
    )(qk_s, conv_prev, v_s, o_s, sg_s, m_s, c_s, n_s, conv_w, conv_b, mh_g)


def _attn_prompt_kernel(q_ref, k_ref, v_ref, k0_ref, v0_ref, o_ref):
    i = pl.program_id(2)
    outs = []
    for hh in range(2):
        lanes = slice(hh * LANE, (hh + 1) * LANE)
        q = q_ref[:, lanes]
        s0 = _dot_nt(q, k0_ref[:, lanes]) * ATTN_SCALE
        m = jnp.max(s0, axis=-1, keepdims=True)
        p0 = jnp.exp(s0 - m)
        l = jnp.sum(p0, axis=-1, keepdims=True)
        acc = _dot(p0.astype(BF16), v0_ref[...])

        def kv_tile(kt, carry, masked):
            m, l, acc = carry
            r0 = pl.multiple_of(kt * ATT_TK, ATT_TK)
            s = _dot_nt(q, k_ref[pl.ds(r0, ATT_TK), lanes]) * ATTN_SCALE
            if masked:
                row = lax.broadcasted_iota(jnp.int32, (ATT_TQ, ATT_TK), 0)
                col = lax.broadcasted_iota(jnp.int32, (ATT_TQ, ATT_TK), 1)
                s = jnp.where(col <= row, s, -jnp.inf)
            m_new = jnp.maximum(m, jnp.max(s, axis=-1, keepdims=True))
            alpha = jnp.exp(m - m_new)
            p = jnp.exp(s - m_new)
            l = alpha * l + jnp.sum(p, axis=-1, keepdims=True)
            acc = alpha * acc + _dot(p.astype(BF16), v_ref[pl.ds(r0, ATT_TK), :])
            return m_new, l, acc

        carry = lax.fori_loop(0, i, functools.partial(kv_tile, masked=False), (m, l, acc))
        m, l, acc = kv_tile(i, carry, masked=True)
        outs.append(acc / l)
    lane = lax.broadcasted_iota(jnp.int32, (ATT_TQ, LANE), 1)
    o_ref[...] = jnp.where(lane < B_DV, outs[0], outs[1]).astype(BF16)


def _attn_prompt(q, k, vb):
    assert ATT_TQ == ATT_TK
    nq = SEQ // ATT_TQ
    meta_blk = META_OFF // N_META
    return pl.pallas_call(
        _attn_prompt_kernel,
        grid=(BATCH, B_HEADS // 2, nq),
        in_specs=[pl.BlockSpec((ATT_TQ, 2 * LANE), lambda b, j, i: (b * nq + i, j)),
                  pl.BlockSpec((SEQ, 2 * LANE), lambda b, j, i: (b, j)),
                  pl.BlockSpec((SEQ, LANE), lambda b, j, i: (b, j)),
                  pl.BlockSpec((N_META, 2 * LANE), lambda b, j, i: (meta_blk + b, j)),
                  pl.BlockSpec((N_META, LANE), lambda b, j, i: (meta_blk + b, j))],
        out_specs=pl.BlockSpec((ATT_TQ, LANE), lambda b, j, i: (b * nq + i, j)),
        out_shape=jax.ShapeDtypeStruct((T_PAD, B_HEADS * B_DV), BF16),
        compiler_params=_cparams(("parallel", "parallel", "parallel")),
        name="attn_prompt",
    )(q, k, vb, k, vb)


def _attn_meta_kernel(q_ref, k_ref, v_ref, o_ref):
    row = lax.broadcasted_iota(jnp.int32, (N_META, N_META), 0)
    col = lax.broadcasted_iota(jnp.int32, (N_META, N_META), 1)
    lane = lax.broadcasted_iota(jnp.int32, (N_META, B_HEADS * B_DV), 1)
    out = jnp.zeros((N_META, B_HEADS * B_DV), F32)
    for h in range(B_HEADS):
        lanes = slice(h * LANE, (h + 1) * LANE)
        s = _dot_nt(q_ref[:, lanes], k_ref[:, lanes]) * ATTN_SCALE
        s = jnp.where(col <= row, s, -jnp.inf)
        p = jnp.exp(s - jnp.max(s, axis=-1, keepdims=True))
        l = jnp.sum(p, axis=-1, keepdims=True)
        o = _dot(p.astype(BF16), v_ref[...]) / l
        out = jnp.where(lane // B_DV == h, o, out)
    o_ref[...] = out.astype(BF16)


def _attn_meta(q, k, vb):
    meta_blk = META_OFF // N_META

    def spec(n):
        return pl.BlockSpec((N_META, n), lambda b: (meta_blk + b, 0))

    return pl.pallas_call(
        _attn_meta_kernel,
        grid=(BATCH,),
        in_specs=[spec(B_HEADS * LANE), spec(B_HEADS * LANE), spec(B_HEADS * B_DV)],
        out_specs=pl.BlockSpec((N_META, B_HEADS * B_DV), lambda b: (b, 0)),
        out_shape=jax.ShapeDtypeStruct((BATCH * N_META, B_HEADS * B_DV), BF16),
        compiler_params=_cparams(("parallel",)),
        name="attn_meta",
    )(q, k, vb)


def _attn_sample_kernel(pt_ref, q_ref, ckvn_ref, sg_ref, wk_ref, er_ref, wuv_ref, *rest):
    ck_refs = rest[:PAGES_PER_STEP]
    kr_refs = rest[PAGES_PER_STEP:2 * PAGES_PER_STEP]
    o_ref, qlat_ref, qr_ref, m_ref, l_ref, acc_ref = rest[2 * PAGES_PER_STEP:]
    p_idx = pl.program_id(1)

    @pl.when(p_idx == 0)
    def _init():
        q = q_ref[...]
        row = lax.broadcasted_iota(jnp.int32, (B_HEADS, B_HEADS * LANE), 0)
        lane = lax.broadcasted_iota(jnp.int32, (B_HEADS, B_HEADS * LANE), 1)
        qbd = jnp.where(lane // LANE == row, jnp.broadcast_to(q, (B_HEADS, B_HEADS * LANE)), 0.0).astype(BF16)
        qlat = _dot(qbd, wk_ref[...])
        qr = _dot(qbd, er_ref[...])
        qlat_ref[...] = qlat.astype(BF16)
        qr_ref[...] = qr.astype(BF16)
        ckvn = ckvn_ref[...]
        s_new = (jnp.sum(qlat * ckvn, axis=-1, keepdims=True)
                 + jnp.sum(qr * sg_ref[...], axis=-1, keepdims=True)) * ATTN_SCALE
        m_ref[...] = s_new
        l_ref[...] = jnp.ones_like(s_new)
        acc_ref[...] = jnp.broadcast_to(ckvn, (B_HEADS, KV_RANK))

    qlat = qlat_ref[...]
    qr = qr_ref[:, ROPE_LANE:ROPE_LANE + B_ROPE]
    cks = [r[...].astype(BF16) for r in ck_refs]
    s = jnp.concatenate([_dot_nt(qlat, ck) + _dot_nt(qr, kr[...].astype(BF16)) for ck, kr in zip(cks, kr_refs)],
                        axis=1) * ATTN_SCALE
    m_old = m_ref[...]
    m_new = jnp.maximum(m_old, jnp.max(s, axis=-1, keepdims=True))
    alpha = jnp.exp(m_old - m_new)
    p = jnp.exp(s - m_new)
    l_ref[...] = alpha * l_ref[...] + jnp.sum(p, axis=-1, keepdims=True)
    pb = p.astype(BF16)
    acc = alpha * acc_ref[...]
    for g, ck in enumerate(cks):
        acc = acc + _dot(pb[:, g * PAGE_SIZE:(g + 1) * PAGE_SIZE], ck)
    acc_ref[...] = acc
    m_ref[...] = m_new

    @pl.when(p_idx == pl.num_programs(1) - 1)
    def _fin():
        o_lat = (acc_ref[...] / l_ref[...]).astype(BF16)
        o_all = _dot(o_lat, wuv_ref[...])
        row = lax.broadcasted_iota(jnp.int32, o_all.shape, 0)
        lane = lax.broadcasted_iota(jnp.int32, o_all.shape, 1)
        o_ref[...] = jnp.sum(jnp.where(lane // B_DV == row, o_all, 0.0), axis=0, keepdims=True)


def _attn_sample(layer, page_table, q_s, ckvn_s, sg_s, wk_ext, e_rope, w_uv, cache_ckv, cache_krope):
    steps = N_PAGES // PAGES_PER_STEP

    def per_b(n):
        return pl.BlockSpec((None, 1, n), lambda b, p, pt: (b, 0, 0))

    def const(shape):
        return pl.BlockSpec(shape, lambda b, p, pt: (0, 0), pipeline_mode=pl.Buffered(1))

    def page(width, g):
        return pl.BlockSpec((None, None, PAGE_SIZE, width),
                            lambda b, p, pt: (layer, pt[b * N_PAGES + p * PAGES_PER_STEP + g], 0, 0))

    grid_spec = pltpu.PrefetchScalarGridSpec(
        num_scalar_prefetch=1,
        grid=(DEC_BATCH, steps),
        in_specs=[per_b(B_HEADS * LANE), per_b(KV_RANK), per_b(LANE),
                  const((B_HEADS * LANE, KV_RANK)), const((B_HEADS * LANE, LANE)), const((KV_RANK, B_HEADS * B_DV))]
                 + [page(KV_RANK, g) for g in range(PAGES_PER_STEP)]
                 + [page(B_ROPE, g) for g in range(PAGES_PER_STEP)],
        out_specs=pl.BlockSpec((None, 1, B_HEADS * B_DV), lambda b, p, pt: (b, 0, 0)),
        scratch_shapes=[pltpu.VMEM((B_HEADS, KV_RANK), BF16), pltpu.VMEM((B_HEADS, LANE), BF16),
                        pltpu.VMEM((B_HEADS, 1), F32), pltpu.VMEM((B_HEADS, 1), F32),
                        pltpu.VMEM((B_HEADS, KV_RANK), F32)],
    )
    return pl.pallas_call(
        _attn_sample_kernel,
        grid_spec=grid_spec,
        out_shape=jax.ShapeDtypeStruct((DEC_BATCH, 1, B_HEADS * B_DV), F32),
        compiler_params=_cparams(("parallel", "arbitrary")),
        name="attn_sample",
    )(page_table, q_s, ckvn_s, sg_s, wk_ext, e_rope, w_uv,
      *([cache_ckv] * PAGES_PER_STEP), *([cache_krope] * PAGES_PER_STEP))


def _merge_kernel(x_ref, hn_ref, ob_ref, ga_ref, gb_ref, wpa_ref, wpb_ref, wo_ref, g_ref, b_ref, o_ref):
    y_a = _dot(hn_ref[...], wpa_ref[...])
    y_b = _dot(ob_ref[...], wpb_ref[...])
    mix = jax.nn.sigmoid(ga_ref[...]) * y_a + jax.nn.sigmoid(gb_ref[...]) * y_b
    y = _dot(mix.astype(BF16), wo_ref[...])
    o_ref[...] = _layer_norm(DN_ALPHA * x_ref[...] + y, g_ref[...], b_ref[...])


def _merge(x, hn, ob, ga, gb, w_pa, w_pb, w_o, g, b):
    def row(n):
        return pl.BlockSpec((TM, n), lambda i: (i, 0))

    return pl.pallas_call(
        _merge_kernel,
        grid=(N_TILES,),
        in_specs=[row(D_MODEL), row(A_VW), row(B_HEADS * B_DV), row(D_MODEL), row(D_MODEL),
                  _const_spec((A_VW, D_MODEL)), _const_spec((B_HEADS * B_DV, D_MODEL)),
                  _const_spec((D_MODEL, D_MODEL)), _const_spec((1, D_MODEL)), _const_spec((1, D_MODEL))],
        out_specs=row(D_MODEL),
        out_shape=jax.ShapeDtypeStruct((T_PAD, D_MODEL), F32),
        compiler_params=_cparams(("parallel",)),
        name="merge_outproj",
    )(x, hn, ob, ga, gb, w_pa, w_pb, w_o, g, b)


def _prep_layer(l, w_in, b_in, q_norm_g, kv_norm_g, w_uq, w_uk, w_uv):
    offs = np.cumsum((0,) + IN_SIZES)

    def grp(a, i):
        return a[..., offs[i]:offs[i + 1]]

    def arrange(a):
        lead = a.shape[:-1]
        z = lambda n: jnp.zeros(lead + (n,), F32)
        sg = jnp.concatenate([grp(a, 6), grp(a, 7), z(ROPE_LANE - 2 * A_HEADS), grp(a, 2),
                              z(LANE - ROPE_LANE - B_ROPE)], axis=-1)
        return jnp.concatenate([grp(a, 0), grp(a, 1), sg, grp(a, 3), grp(a, 4), grp(a, 5), grp(a, 8), grp(a, 9)],
                               axis=-1)

    pad_q = LANE - B_NOPE - B_ROPE
    wuq = jnp.pad(w_uq[l], ((0, 0), (0, 0), (0, pad_q))).reshape(Q_RANK, B_HEADS * LANE)
    wuk = jnp.pad(w_uk[l], ((0, 0), (0, 0), (0, LANE - B_NOPE)))
    place = np.zeros((LANE, B_HEADS, LANE), np.float32)
    for e in range(B_ROPE):
        place[ROPE_LANE + e, :, ROPE_LANE + e] = 1.0
    w_kf = jnp.concatenate([wuk.reshape(KV_RANK, B_HEADS * LANE),
                            jnp.asarray(place).reshape(LANE, B_HEADS * LANE)], axis=0)
    wk_ext = jnp.transpose(wuk, (1, 2, 0)).reshape(B_HEADS * LANE, KV_RANK)
    return {
        "w_in": arrange(w_in[l]).astype(BF16),
        "b_in": arrange(b_in[l])[None, :],
        "qg": q_norm_g[l][None, :],
        "kvg": kv_norm_g[l][None, :],
        "w_uq": wuq.astype(BF16),
        "w_kf": w_kf.astype(BF16),
        "w_uv": w_uv[l].reshape(KV_RANK, B_HEADS * B_DV).astype(BF16),
        "wk_ext": wk_ext.astype(BF16),
    }


def _rope_select():
    e = np.zeros((B_HEADS, LANE, LANE), np.float32)
    for r in range(B_ROPE):
        e[:, ROPE_LANE + r, ROPE_LANE + r] = 1.0
    return jnp.asarray(e.reshape(B_HEADS * LANE, LANE)).astype(BF16)


def _rope_tabs():
    pos_real = N_META + np.arange(SEQ)
    pos_tail = np.zeros((TM,), np.int64)
    pos_tail[:BATCH * N_META] = np.arange(BATCH * N_META) % N_META
    pos_tail[BATCH * N_META:TAIL_ROWS] = PAST_LEN
    pos = jnp.asarray(np.concatenate([pos_real, pos_tail]), dtype=jnp.int32)
    inv = ROPE_BASE ** (-jnp.arange(0, B_ROPE, 2, dtype=F32) / B_ROPE)
    ang = pos.astype(F32)[:, None] * inv[None, :]
    cos, sin = jnp.cos(ang), jnp.sin(ang)
    n = pos.shape[0]
    z = lambda w: jnp.zeros((n, w), F32)
    tail = LANE - ROPE_LANE - B_ROPE
    tc = jnp.concatenate([jnp.ones((n, ROPE_LANE), F32), cos, cos, z(tail)], axis=1)
    ts1 = jnp.concatenate([z(ROPE_LANE), -sin, z(HALF_ROPE), z(tail)], axis=1)
    ts2 = jnp.concatenate([z(ROPE_LANE), z(HALF_ROPE), sin, z(tail)], axis=1)
    return tc, ts1, ts2


def _with_tail(main, meta_rows, samp_rows):
    width = main.shape[1]
    tail = jnp.concatenate([meta_rows.astype(main.dtype), samp_rows.astype(main.dtype),
                            jnp.zeros((TM - TAIL_ROWS, width), main.dtype)], axis=0)
    return lax.dynamic_update_slice(main, tail, (META_OFF, 0))


def kernel(x_prompt, x_sample, cache_ckv, cache_krope, page_table, state_C, state_n, state_m, state_conv, meta,
           w_in, b_in, conv_w, conv_b, mh_g, q_norm_g, kv_norm_g, w_uq, w_uk, w_uv, w_pa, w_pb, w_o,
           ffn1_up, ffn1_down, ffn2_up, ffn2_down, ln_g, ln_b):
    x = jnp.concatenate([x_prompt.reshape(N_REAL, D_MODEL),
                         jnp.tile(meta.astype(F32), (BATCH, 1)),
                         x_sample.reshape(DEC_BATCH, D_MODEL),
                         jnp.zeros((TM - TAIL_ROWS, D_MODEL), F32)], axis=0)
    tabs = _rope_tabs()
    e_rope = _rope_select()
    pt_flat = page_table.reshape(-1)
    samp = slice(SAMP_OFF, SAMP_OFF + DEC_BATCH)
    metas = slice(META_OFF, SAMP_OFF)

    def seq_order(a, width):
        return jnp.concatenate([a[metas].reshape(BATCH, N_META, width), a[:N_REAL].reshape(BATCH, SEQ, width)],
                               axis=1)

    st = {k: [] for k in ("ckv_p", "kr_p", "c_p", "n_p", "m_p", "conv_p", "ckv_s", "kr_s", "c_s", "n_s", "m_s",
                          "conv_s")}
    for l in range(DEPTH):
        wp = _prep_layer(l, w_in, b_in, q_norm_g, kv_norm_g, w_uq, w_uk, w_uv)
        x = _ffn(x, ffn1_up[l].astype(BF16), ffn1_down[l].astype(BF16), ln_g[l, 0][None], ln_b[l, 0][None])
        q, k, vb, ckvn, sg, qk, vm, om, ga, gb = _inproj(x, wp, tabs)

        gates = seq_order(sg[:, :2 * A_HEADS], 2 * A_HEADS)
        gates = gates.reshape(BATCH, S_ALL, 2, A_HEADS)
        gr = jnp.transpose(gates, (0, 3, 2, 1))
        gc = jnp.transpose(gates, (0, 3, 1, 2))
        hn, hn_meta, cx, m_p = _mlstm_prompt(qk, vm, om, gr, gc, conv_w[l], conv_b[l][None], mh_g[l][None])
        hn_s, c_s, n_s, m_s = _mlstm_sample(
            qk[samp][:, None, :], state_conv[l], vm[samp].astype(F32)[:, None, :], om[samp][:, None, :],
            sg[samp][:, None, :], state_m[l][:, None, :], state_C[l], state_n[l],
            conv_w[l], conv_b[l][None], mh_g[l][None])
        hn = _with_tail(hn, hn_meta, hn_s.reshape(DEC_BATCH, A_VW))

        ob = _attn_prompt(q, k, vb)
        ob_meta = _attn_meta(q, k, vb)
        ob_s = _attn_sample(l, pt_flat, q[samp].astype(F32)[:, None, :], ckvn[samp][:, None, :],
                            sg[samp][:, None, :], wp["wk_ext"], e_rope, wp["w_uv"], cache_ckv, cache_krope)
        ob = _with_tail(ob, ob_meta, ob_s.reshape(DEC_BATCH, B_HEADS * B_DV))

        x = _merge(x, hn, ob, ga, gb, w_pa[l].astype(BF16), w_pb[l].astype(BF16), w_o[l].astype(BF16),
                   ln_g[l, 1][None], ln_b[l, 1][None])
        x = _ffn(x, ffn2_up[l].astype(BF16), ffn2_down[l].astype(BF16), ln_g[l, 2][None], ln_b[l, 2][None])

        kr_all = sg[:, ROPE_LANE:ROPE_LANE + B_ROPE]
        st["ckv_p"].append(seq_order(ckvn, KV_RANK))
        st["kr_p"].append(seq_order(kr_all, B_ROPE))
        st["c_p"].append(cx[..., :A_DV])
        st["n_p"].append(cx[..., A_DV])
        st["m_p"].append(m_p[:, :, 0, 0])
        st["conv_p"].append(qk[:N_REAL].reshape(BATCH, SEQ, 2 * A_QK)[:, SEQ - (CONV_W - 1):])
        st["ckv_s"].append(ckvn[samp][:, None, :])
        st["kr_s"].append(kr_all[samp][:, None, :])
        st["c_s"].append(c_s)
        st["n_s"].append(n_s)
        st["m_s"].append(m_s[:, 0, :])
        st["conv_s"].append(jnp.concatenate([state_conv[l][:, 1:], qk[samp][:, None, :]], axis=1))

    y_prompt = x[:N_REAL].reshape(BATCH, SEQ, D_MODEL)
    y_sample = x[samp].reshape(DEC_BATCH, 1, D_MODEL)
    s = {k: jnp.stack(v) for k, v in st.items()}
    return (y_prompt, y_sample, s["ckv_p"], s["kr_p"], s["c_p"], s["n_p"], s["m_p"], s["conv_p"],
            s["ckv_s"], s["kr_s"], s["c_s"], s["n_s"], s["m_s"], s["conv_s"])
```

```python
import functools

import numpy as np
import jax
import jax.numpy as jnp
from jax import lax
from jax.experimental import pallas as pl
from jax.experimental.pallas import tpu as pltpu

F32 = jnp.float32
BF16 = jnp.bfloat16

D_MODEL = 1024
BATCH = 8
SEQ = 2048
DEPTH = 4
DEC_BATCH = 32
PAST_LEN = 16384
PAGE_SIZE = 128
N_PAGES = PAST_LEN // PAGE_SIZE
N_META = 16
A_HEADS = 4
A_DK = 128
A_DV = 128
A_QK = A_HEADS * A_DK
A_VW = A_HEADS * A_DV
CONV_W = 4
B_HEADS = 8
B_NOPE = 64
B_ROPE = 32
B_DV = 64
Q_RANK = 384
KV_RANK = 256
ROPE_BASE = 10000.0
ATTN_SCALE = (B_NOPE + B_ROPE) ** -0.5
LOG2E = 1.4426950408889634
Q_SCALE = ATTN_SCALE * LOG2E
D_FF = 2816
DN_ALPHA = (2 * DEPTH) ** 0.25
LN_EPS = 1e-5
RMS_EPS = 1e-6
IN_SIZES = (Q_RANK, KV_RANK, B_ROPE, 2 * A_QK, A_VW, A_VW, A_HEADS, A_HEADS, D_MODEL, D_MODEL)

LANE = 128
HALF_ROPE = B_ROPE // 2
ROPE_LANE = B_NOPE

TM = 512
N_REAL = BATCH * SEQ
META_OFF = N_REAL
SAMP_OFF = META_OFF + BATCH * N_META
TAIL_ROWS = BATCH * N_META + DEC_BATCH
T_PAD = N_REAL + TM
N_TILES = T_PAD // TM
REAL_TILES_PER_SEQ = SEQ // TM
S_ALL = SEQ + N_META

G_CQ, G_CKV, G_SG, G_QK, G_V, G_O, G_GA, G_GB = 0, 384, 640, 768, 1792, 2304, 2816, 3840
N_IN_P = 4864

FF_CHUNKS = 2
FF_C = D_FF // FF_CHUNKS

MLSTM_L = 256
CONV_PAD = 8

ATT_TQ = 512
ATT_TK = 512
PAGES_PER_STEP = 32
KEYS_PER_STEP = PAGES_PER_STEP * PAGE_SIZE

VMEM_LIMIT = 56 * 1024 * 1024


def _cparams(sem):
    return pltpu.CompilerParams(dimension_semantics=sem, vmem_limit_bytes=VMEM_LIMIT)


def _const_spec(shape):
    nd = len(shape)
    return pl.BlockSpec(shape, lambda *_: (0,) * nd, pipeline_mode=pl.Buffered(1))


def _layer_spec(layer, shape):
    nd = len(shape)
    return pl.BlockSpec((None,) + shape, lambda *_: (layer,) + (0,) * nd, pipeline_mode=pl.Buffered(1))


def _layer_norm(z, g, b):
    mu = jnp.mean(z, axis=-1, keepdims=True)
    zc = z - mu
    var = jnp.mean(zc * zc, axis=-1, keepdims=True)
    return zc * lax.rsqrt(var + LN_EPS) * g + b


def _rms_norm(z, g):
    return z * lax.rsqrt(jnp.mean(z * z, axis=-1, keepdims=True) + RMS_EPS) * g


def _dot(a, b):
    return jnp.dot(a, b, preferred_element_type=F32)


def _dot_nt(a, b):
    return lax.dot_general(a, b, (((1,), (1,)), ((), ())), preferred_element_type=F32)


def _dot_tn(a, b):
    return lax.dot_general(a, b, (((0,), (0,)), ((), ())), preferred_element_type=F32)


def _log_sigmoid(x):
    return jnp.minimum(x, 0.0) - jnp.log1p(jnp.exp(-jnp.abs(x)))


def _ffn_kernel(x_ref, wup_ref, wdn_ref, g_ref, b_ref, o_ref):
    x = x_ref[...]
    xb = x.astype(BF16)
    y = jnp.zeros_like(x)
    for c in range(FF_CHUNKS):
        lo = c * FF_C
        a = _dot(xb, wup_ref[:, lo:lo + FF_C])
        u = _dot(xb, wup_ref[:, D_FF + lo:D_FF + lo + FF_C])
        h = (a * jax.nn.sigmoid(a) * u).astype(BF16)
        y = y + _dot(h, wdn_ref[lo:lo + FF_C, :])
    o_ref[...] = _layer_norm(DN_ALPHA * x + 0.5 * y, g_ref[...], b_ref[...])


def _ffn(x, layer, w_up, w_down, ln_idx, g, b):
    row = pl.BlockSpec((TM, D_MODEL), lambda i: (i, 0))
    return pl.pallas_call(
        _ffn_kernel,
        grid=(N_TILES,),
        in_specs=[row, _layer_spec(layer, (D_MODEL, 2 * D_FF)), _layer_spec(layer, (D_FF, D_MODEL)),
                  _layer_spec(ln_idx, (1, D_MODEL)), _layer_spec(ln_idx, (1, D_MODEL))],
        out_specs=row,
        out_shape=jax.ShapeDtypeStruct((T_PAD, D_MODEL), F32),
        compiler_params=_cparams(("parallel",)),
        name="ffn_postnorm",
    )(x, w_up, w_down, g, b)


def _rope(z, tc, ts1, ts2):
    return z * tc + pltpu.roll(z, LANE - HALF_ROPE, 1) * ts1 + pltpu.roll(z, HALF_ROPE, 1) * ts2


def _inproj_kernel(x_ref, w_ref, b_ref, qg_ref, kvg_ref, wuq_ref, wkf_ref, wuv_ref, tc_ref, ts1_ref, ts2_ref,
                   q_ref, k_ref, vb_ref, ckv_ref, sg_ref, qk_ref, vm_ref, om_ref, ga_ref, gb_ref):
    xb = x_ref[...].astype(BF16)

    def proj(off, n):
        return _dot(xb, w_ref[:, off:off + n]) + b_ref[:, off:off + n]

    tc, ts1, ts2 = tc_ref[...], ts1_ref[...], ts2_ref[...]
    cqn = _rms_norm(proj(G_CQ, Q_RANK), qg_ref[...]).astype(BF16)
    q = _dot(cqn, wuq_ref[...])
    for h in range(B_HEADS):
        qh = _rope(q[:, h * LANE:(h + 1) * LANE], tc, ts1, ts2)
        q_ref[:, h * LANE:(h + 1) * LANE] = (qh * Q_SCALE).astype(BF16)
    ckvn = _rms_norm(proj(G_CKV, KV_RANK), kvg_ref[...])
    ckv_ref[...] = ckvn
    sg = _rope(proj(G_SG, LANE), tc, ts1, ts2)
    lane = lax.broadcasted_iota(jnp.int32, sg.shape, 1)
    sg = jnp.where((lane >= A_HEADS) & (lane < 2 * A_HEADS), _log_sigmoid(sg), sg)
    sg_ref[...] = sg
    kin = jnp.concatenate([ckvn.astype(BF16), sg.astype(BF16)], axis=1)
    k_ref[...] = _dot(kin, wkf_ref[...]).astype(BF16)
    vb_ref[...] = _dot(kin[:, :KV_RANK], wuv_ref[...]).astype(BF16)
    qk_ref[...] = proj(G_QK, 2 * A_QK)
    vm_ref[...] = proj(G_V, A_VW).astype(BF16)
    om_ref[...] = proj(G_O, A_VW)
    ga_ref[...] = proj(G_GA, D_MODEL)
    gb_ref[...] = proj(G_GB, D_MODEL)


def _inproj(x, wp, tabs):
    def row(n):
        return pl.BlockSpec((TM, n), lambda i: (i, 0))

    tab = pl.BlockSpec((TM, LANE), lambda i: (jnp.where(i < N_REAL // TM, i % REAL_TILES_PER_SEQ,
                                                       REAL_TILES_PER_SEQ), 0))
    widths = (B_HEADS * LANE, B_HEADS * LANE, B_HEADS * B_DV, KV_RANK, LANE, 2 * A_QK, A_VW, A_VW, D_MODEL, D_MODEL)
    dtypes = (BF16, BF16, BF16, F32, F32, F32, BF16, F32, F32, F32)
    return pl.pallas_call(
        _inproj_kernel,
        grid=(N_TILES,),
        in_specs=[row(D_MODEL), _const_spec((D_MODEL, N_IN_P)), _const_spec((1, N_IN_P)),
                  _const_spec((1, Q_RANK)), _const_spec((1, KV_RANK)),
                  _const_spec((Q_RANK, B_HEADS * LANE)), _const_spec((KV_RANK + LANE, B_HEADS * LANE)),
                  _const_spec((KV_RANK, B_HEADS * B_DV)), tab, tab, tab],
        out_specs=[row(n) for n in widths],
        out_shape=[jax.ShapeDtypeStruct((T_PAD, n), d) for n, d in zip(widths, dtypes)],
        compiler_params=_cparams(("parallel",)),
        name="in_proj",
    )(x, wp["w_in"], wp["b_in"], wp["qg"], wp["kvg"], wp["w_uq"], wp["w_kf"], wp["w_uv"], *tabs)


def _mlstm_prompt_kernel(qm_ref, km_ref, q0_ref, k0_ref, vm_ref, v0_ref, om_ref, o0_ref, gr_ref, gc_ref,
                         cwq_ref, cwk_ref, cbq_ref, cbk_ref, mhg_ref,
                         hm_ref, h0_ref, cx_ref, m_ref, uq_ref, uk_ref):
    for u_ref, a0_ref, am_ref in ((uq_ref, q0_ref, qm_ref), (uk_ref, k0_ref, km_ref)):
        u_ref[0:CONV_PAD, :] = jnp.zeros((CONV_PAD, LANE), F32)
        u_ref[CONV_PAD:CONV_PAD + N_META, :] = a0_ref[...]
        u_ref[CONV_PAD + N_META:, :] = am_ref[...]

    def conv_silu(u_ref, w_ref, b_ref, t0, length):
        acc = b_ref[...]
        for j in range(CONV_W):
            lo = CONV_PAD - (CONV_W - 1) + j + t0
            acc = acc + w_ref[j:j + 1, :] * u_ref[lo:lo + length, :]
        return acc * jax.nn.sigmoid(acc)

    def chunk(t0, length, v_ref, o_ref, h_ref, r0, cx, m):
        q = conv_silu(uq_ref, cwq_ref, cbq_ref, t0, length)
        k = conv_silu(uk_ref, cwk_ref, cbk_ref, t0, length) * (A_DK ** -0.5)
        qb, kb = q.astype(BF16), k.astype(BF16)
        ig_r = gr_ref[0:1, t0:t0 + length]
        lf_r = gr_ref[1:2, t0:t0 + length]
        ig_c = gc_ref[t0:t0 + length, 0:1]
        lf_c = gc_ref[t0:t0 + length, 1:2]
        row = lax.broadcasted_iota(jnp.int32, (length, length), 0)
        col = lax.broadcasted_iota(jnp.int32, (length, length), 1)
        tri = row >= col
        b_c = jnp.sum(jnp.where(tri, lf_r, 0.0), axis=1, keepdims=True)
        b_r = jnp.sum(jnp.where(row <= col, lf_c, 0.0), axis=0, keepdims=True)
        dmat = jnp.where(tri, b_c - b_r + ig_r, -jnp.inf)
        inter = b_c + m
        m_row = jnp.maximum(inter, jnp.max(dmat, axis=1, keepdims=True))
        s = _dot_nt(qb, kb) * jnp.exp(dmat - m_row)
        w_prev = jnp.exp(inter - m_row)
        lane = lax.broadcasted_iota(jnp.int32, (length, LANE), 1)
        ve = jnp.concatenate([v_ref[r0:r0 + length, :], (lane == 0).astype(BF16)], axis=1)
        tot = w_prev * _dot(qb, cx.astype(BF16)) + _dot(s.astype(BF16), ve)
        h = tot[:, :A_DV] / jnp.maximum(jnp.abs(tot[:, A_DV:A_DV + 1]), jnp.exp(-m_row))
        b_last = b_c[length - 1:length, :]
        g = b_last - b_c + ig_c
        m_new = jnp.maximum(b_last + m, jnp.max(g, axis=0, keepdims=True))
        decay = jnp.exp(b_last + m - m_new)
        wk = (jnp.exp(g - m_new) * k).astype(BF16)
        cx_new = decay * cx + _dot_tn(wk, ve)
        mu = jnp.mean(h, axis=-1, keepdims=True)
        hc = h - mu
        var = jnp.mean(hc * hc, axis=-1, keepdims=True)
        hn = hc * lax.rsqrt(var + LN_EPS) * mhg_ref[...] * jax.nn.sigmoid(o_ref[r0:r0 + length, :])
        h_ref[r0:r0 + length, :] = hn.astype(BF16)
        return cx_new, m_new

    cx = jnp.zeros((A_DK, 2 * LANE), F32)
    m = jnp.zeros((1, 1), F32)
    cx, m = chunk(0, N_META, v0_ref, o0_ref, h0_ref, 0, cx, m)
    for c in range(SEQ // MLSTM_L):
        cx, m = chunk(N_META + c * MLSTM_L, MLSTM_L, vm_ref, om_ref, hm_ref, c * MLSTM_L, cx, m)
    cx_ref[...] = cx
    m_ref[...] = jnp.broadcast_to(m, (1, LANE))


def _mlstm_prompt(qk, vm, om, gr, gc, conv_w, conv_b, mh_g):
    meta_blk = META_OFF // N_META

    def main(off):
        return pl.BlockSpec((SEQ, LANE), lambda b, h: (b, h + off))

    def meta(off):
        return pl.BlockSpec((N_META, LANE), lambda b, h: (meta_blk + b, h + off))

    def wcol(rows, off):
        return pl.BlockSpec((rows, LANE), lambda b, h: (0, h + off))

    return pl.pallas_call(
        _mlstm_prompt_kernel,
        grid=(BATCH, A_HEADS),
        in_specs=[main(0), main(A_HEADS), meta(0), meta(A_HEADS), main(0), meta(0), main(0), meta(0),
                  pl.BlockSpec((None, None, 2, S_ALL), lambda b, h: (b, h, 0, 0)),
                  pl.BlockSpec((None, None, S_ALL, 2), lambda b, h: (b, h, 0, 0)),
                  wcol(CONV_W, 0), wcol(CONV_W, A_HEADS), wcol(1, 0), wcol(1, A_HEADS), wcol(1, 0)],
        out_specs=[main(0),
                   pl.BlockSpec((N_META, LANE), lambda b, h: (b, h)),
                   pl.BlockSpec((None, None, A_DK, 2 * LANE), lambda b, h: (b, h, 0, 0)),
                   pl.BlockSpec((None, None, 1, LANE), lambda b, h: (b, h, 0, 0))],
        out_shape=[jax.ShapeDtypeStruct((T_PAD, A_VW), BF16),
                   jax.ShapeDtypeStruct((BATCH * N_META, A_VW), BF16),
                   jax.ShapeDtypeStruct((BATCH, A_HEADS, A_DK, 2 * LANE), F32),
                   jax.ShapeDtypeStruct((BATCH, A_HEADS, 1, LANE), F32)],
        scratch_shapes=[pltpu.VMEM((CONV_PAD + S_ALL, LANE), F32), pltpu.VMEM((CONV_PAD + S_ALL, LANE), F32)],
        compiler_params=_cparams(("parallel", "parallel")),
        name="mlstm_prompt",
    )(qk, qk, qk, qk, vm, vm, om, om, gr, gc, conv_w, conv_w, conv_b, conv_b, mh_g)


def _mlstm_sample_kernel(qk_ref, cprev_ref, v_ref, o_ref, sg_ref, m_ref, c_ref, n_ref, cw_ref, cb_ref, mhg_ref,
                         hn_ref, cnew_ref, nnew_ref, mnew_ref):
    acc = (cb_ref[...] + jnp.sum(cw_ref[0:CONV_W - 1, :] * cprev_ref[...], axis=0, keepdims=True)
           + cw_ref[CONV_W - 1:CONV_W, :] * qk_ref[...])
    a = acc * jax.nn.sigmoid(acc)
    sg = sg_ref[...]
    m_in = m_ref[...]
    row8 = lax.broadcasted_iota(jnp.int32, (8, LANE), 0)
    for h in range(A_HEADS):
        q = a[:, h * A_DK:(h + 1) * A_DK]
        k = a[:, A_QK + h * A_DK:A_QK + (h + 1) * A_DK] * (A_DK ** -0.5)
        v = v_ref[:, h * A_DV:(h + 1) * A_DV]
        ig = sg[:, h:h + 1]
        lf = sg[:, A_HEADS + h:A_HEADS + h + 1]
        m = m_in[:, h:h + 1]
        c = c_ref[h]
        n = n_ref[h:h + 1, :]
        inter = lf + m
        m_row = jnp.maximum(inter, ig)
        s = jnp.sum(q * k, axis=-1, keepdims=True) * jnp.exp(ig - m_row)
        w_prev = jnp.exp(inter - m_row)
        q8 = jnp.broadcast_to(q, (8, A_DK)).astype(BF16)
        qc = _dot(q8, c.astype(BF16))[0:1, :]
        num = w_prev * qc + s * v
        den = w_prev * jnp.sum(q * n, axis=-1, keepdims=True) + s
        hh = num / jnp.maximum(jnp.abs(den), jnp.exp(-m_row))
        wk = jnp.exp(ig - m_row) * k
        wk8 = jnp.where(row8 == 0, jnp.broadcast_to(wk, (8, A_DK)), 0.0).astype(BF16)
        v8 = jnp.broadcast_to(v, (8, A_DV)).astype(BF16)
        cnew_ref[h] = w_prev * c + _dot_tn(wk8, v8)
        nnew_ref[h:h + 1, :] = w_prev * n + wk
        mnew_ref[:, h:h + 1] = m_row
        mu = jnp.mean(hh, axis=-1, keepdims=True)
        hc = hh - mu
        var = jnp.mean(hc * hc, axis=-1, keepdims=True)
        hn = hc * lax.rsqrt(var + LN_EPS) * mhg_ref[:, h * A_DV:(h + 1) * A_DV]
        hn_ref[:, h * A_DV:(h + 1) * A_DV] = hn * jax.nn.sigmoid(o_ref[:, h * A_DV:(h + 1) * A_DV])


def _mlstm_sample(layer, qk_s, conv_prev, v_s, o_s, sg_s, m_s, state_c, state_n, conv_w, conv_b, mh_g):
    def per_b(*shape):
        nd = len(shape)
        return pl.BlockSpec((None,) + shape, lambda b: (b,) + (0,) * nd)

    def per_lb(*shape):
        nd = len(shape)
        return pl.BlockSpec((None, None) + shape, lambda b: (layer, b) + (0,) * nd)

    return pl.pallas_call(
        _mlstm_sample_kernel,
        grid=(DEC_BATCH,),
        in_specs=[per_b(1, 2 * A_QK), per_lb(CONV_W - 1, 2 * A_QK), per_b(1, A_VW), per_b(1, A_VW), per_b(1, LANE),
                  per_lb(1, A_HEADS), per_lb(A_HEADS, A_DK, A_DV), per_lb(A_HEADS, A_DK),
                  _const_spec((CONV_W, 2 * A_QK)), _const_spec((1, 2 * A_QK)), _const_spec((1, A_VW))],
        out_specs=[per_b(1, A_VW), per_b(A_HEADS, A_DK, A_DV), per_b(A_HEADS, A_DK), per_b(1, A_HEADS)],
        out_shape=[jax.ShapeDtypeStruct((DEC_BATCH, 1, A_VW), F32),
                   jax.ShapeDtypeStruct((DEC_BATCH, A_HEADS, A_DK, A_DV), F32),
                   jax.ShapeDtypeStruct((DEC_BATCH, A_HEADS, A_DK), F32),
                   jax.ShapeDtypeStruct((DEC_BATCH, 1, A_HEADS), F32)],
        compiler_params=_cparams(("parallel",)),
        name="mlstm_sample",
    )(qk_s, conv_prev, v_s, o_s, sg_s, m_s, state_c, state_n, conv_w, conv_b, mh_g)


def _attn_prompt_kernel(q_ref, k_ref, v_ref, k0_ref, v0_ref, o_ref):
    i = pl.program_id(2)
    sub = ATT_TQ // ATT_TK
    head_lanes = [slice(hh * LANE, (hh + 1) * LANE) for hh in range(2)]

    def head_tile(lanes, state, r0, diag):
        m, l, acc = state
        s = _dot_nt(q_ref[:, lanes], k_ref[pl.ds(r0, ATT_TK), lanes])
        if diag is not None:
            row = lax.broadcasted_iota(jnp.int32, (ATT_TQ, ATT_TK), 0)
            col = lax.broadcasted_iota(jnp.int32, (ATT_TQ, ATT_TK), 1)
            s = jnp.where(col + diag * ATT_TK <= row, s, -jnp.inf)
        m_new = jnp.maximum(m, jnp.max(s, axis=-1, keepdims=True))
        alpha = jnp.exp2(m - m_new)
        p = jnp.exp2(s - m_new)
        l = alpha * l + jnp.sum(p, axis=-1, keepdims=True)
        acc = alpha * acc + _dot(p.astype(BF16), v_ref[pl.ds(r0, ATT_TK), :])
        return m_new, l, acc

    def kv_tile(kt, carry, diag):
        r0 = pl.multiple_of(kt * ATT_TK, ATT_TK)
        return tuple(head_tile(lanes, st, r0, diag) for lanes, st in zip(head_lanes, carry))

    init = []
    for lanes in head_lanes:
        s0 = _dot_nt(q_ref[:, lanes], k0_ref[:, lanes])
        m = jnp.max(s0, axis=-1, keepdims=True)
        p0 = jnp.exp2(s0 - m)
        init.append((m, jnp.sum(p0, axis=-1, keepdims=True), _dot(p0.astype(BF16), v0_ref[...])))
    carry = lax.fori_loop(0, i * sub, functools.partial(kv_tile, diag=None), tuple(init))
    for d in range(sub):
        carry = kv_tile(i * sub + d, carry, d)
    outs = [acc / l for _, l, acc in carry]
    lane = lax.broadcasted_iota(jnp.int32, (ATT_TQ, LANE), 1)
    o_ref[...] = jnp.where(lane < B_DV, outs[0], outs[1]).astype(BF16)


def _attn_prompt(q, k, vb):
    assert ATT_TQ % ATT_TK == 0
    nq = SEQ // ATT_TQ
    meta_blk = META_OFF // N_META
    return pl.pallas_call(
        _attn_prompt_kernel,
        grid=(BATCH, B_HEADS // 2, nq),
        in_specs=[pl.BlockSpec((ATT_TQ, 2 * LANE), lambda b, j, i: (b * nq + i, j)),
                  pl.BlockSpec((SEQ, 2 * LANE), lambda b, j, i: (b, j)),
                  pl.BlockSpec((SEQ, LANE), lambda b, j, i: (b, j)),
                  pl.BlockSpec((N_META, 2 * LANE), lambda b, j, i: (meta_blk + b, j)),
                  pl.BlockSpec((N_META, LANE), lambda b, j, i: (meta_blk + b, j))],
        out_specs=pl.BlockSpec((ATT_TQ, LANE), lambda b, j, i: (b * nq + i, j)),
        out_shape=jax.ShapeDtypeStruct((T_PAD, B_HEADS * B_DV), BF16),
        compiler_params=_cparams(("parallel", "parallel", "parallel")),
        name="attn_prompt",
    )(q, k, vb, k, vb)


def _attn_meta_kernel(q_ref, k_ref, v_ref, o_ref):
    row = lax.broadcasted_iota(jnp.int32, (N_META, N_META), 0)
    col = lax.broadcasted_iota(jnp.int32, (N_META, N_META), 1)
    lane = lax.broadcasted_iota(jnp.int32, (N_META, B_HEADS * B_DV), 1)
    out = jnp.zeros((N_META, B_HEADS * B_DV), F32)
    for h in range(B_HEADS):
        lanes = slice(h * LANE, (h + 1) * LANE)
        s = _dot_nt(q_ref[:, lanes], k_ref[:, lanes])
        s = jnp.where(col <= row, s, -jnp.inf)
        p = jnp.exp2(s - jnp.max(s, axis=-1, keepdims=True))
        l = jnp.sum(p, axis=-1, keepdims=True)
        o = _dot(p.astype(BF16), v_ref[...]) / l
        out = jnp.where(lane // B_DV == h, o, out)
    o_ref[...] = out.astype(BF16)


def _attn_meta(q, k, vb):
    meta_blk = META_OFF // N_META

    def spec(n):
        return pl.BlockSpec((N_META, n), lambda b: (meta_blk + b, 0))

    return pl.pallas_call(
        _attn_meta_kernel,
        grid=(BATCH,),
        in_specs=[spec(B_HEADS * LANE), spec(B_HEADS * LANE), spec(B_HEADS * B_DV)],
        out_specs=pl.BlockSpec((N_META, B_HEADS * B_DV), lambda b: (b, 0)),
        out_shape=jax.ShapeDtypeStruct((BATCH * N_META, B_HEADS * B_DV), BF16),
        compiler_params=_cparams(("parallel",)),
        name="attn_meta",
    )(q, k, vb)


def _attn_sample_kernel(pt_ref, q_ref, ckvn_ref, sg_ref, wk_ref, er_ref, wuv_ref, *rest):
    ck_refs = rest[:PAGES_PER_STEP]
    kr_refs = rest[PAGES_PER_STEP:2 * PAGES_PER_STEP]
    o_ref, qlat_ref, qr_ref, m_ref, l_ref, acc_ref, kb_ref, krb_ref = rest[2 * PAGES_PER_STEP:]
    p_idx = pl.program_id(1)

    @pl.when(p_idx == 0)
    def _init():
        q = q_ref[...]
        row = lax.broadcasted_iota(jnp.int32, (B_HEADS, B_HEADS * LANE), 0)
        lane = lax.broadcasted_iota(jnp.int32, (B_HEADS, B_HEADS * LANE), 1)
        qbd = jnp.where(lane // LANE == row, jnp.broadcast_to(q, (B_HEADS, B_HEADS * LANE)), 0.0).astype(BF16)
        qlat = _dot(qbd, wk_ref[...])
        qr = _dot(qbd, er_ref[...])
        qlat_ref[...] = qlat.astype(BF16)
        qr_ref[...] = qr.astype(BF16)
        ckvn = ckvn_ref[...]
        s_new = (jnp.sum(qlat * ckvn, axis=-1, keepdims=True)
                 + jnp.sum(qr * sg_ref[...], axis=-1, keepdims=True))
        m_ref[...] = s_new
        l_ref[...] = jnp.ones_like(s_new)
        acc_ref[...] = jnp.broadcast_to(ckvn, (B_HEADS, KV_RANK))

    for g in range(PAGES_PER_STEP):
        kb_ref[g * PAGE_SIZE:(g + 1) * PAGE_SIZE, :] = ck_refs[g][...].astype(BF16)
        krb_ref[:, g * PAGE_SIZE:(g + 1) * PAGE_SIZE] = kr_refs[g][...].astype(BF16)
    qr = qr_ref[:, ROPE_LANE:ROPE_LANE + B_ROPE]
    s = _dot_nt(qlat_ref[...], kb_ref[...]) + _dot(qr, krb_ref[...])
    m_old = m_ref[...]
    m_new = jnp.maximum(m_old, jnp.max(s, axis=-1, keepdims=True))
    alpha = jnp.exp2(m_old - m_new)
    p = jnp.exp2(s - m_new)
    l_ref[...] = alpha * l_ref[...] + jnp.sum(p, axis=-1, keepdims=True)
    acc_ref[...] = alpha * acc_ref[...] + _dot(p.astype(BF16), kb_ref[...])
    m_ref[...] = m_new

    @pl.when(p_idx == pl.num_programs(1) - 1)
    def _fin():
        o_lat = (acc_ref[...] / l_ref[...]).astype(BF16)
        o_all = _dot(o_lat, wuv_ref[...])
        row = lax.broadcasted_iota(jnp.int32, o_all.shape, 0)
        lane = lax.broadcasted_iota(jnp.int32, o_all.shape, 1)
        o_ref[...] = jnp.sum(jnp.where(lane // B_DV == row, o_all, 0.0), axis=0, keepdims=True)


def _attn_sample(layer, page_table, q_s, ckvn_s, sg_s, wk_ext, e_rope, w_uv, cache_ckv, cache_krope_t):
    steps = N_PAGES // PAGES_PER_STEP

    def per_b(n):
        return pl.BlockSpec((None, 1, n), lambda b, p, pt: (b, 0, 0))

    def const(shape):
        return pl.BlockSpec(shape, lambda b, p, pt: (0, 0), pipeline_mode=pl.Buffered(1))

    def page(rows, width, g):
        return pl.BlockSpec((None, None, rows, width),
                            lambda b, p, pt: (layer, pt[b * N_PAGES + p * PAGES_PER_STEP + g], 0, 0))

    grid_spec = pltpu.PrefetchScalarGridSpec(
        num_scalar_prefetch=1,
        grid=(DEC_BATCH, steps),
        in_specs=[per_b(B_HEADS * LANE), per_b(KV_RANK), per_b(LANE),
                  const((B_HEADS * LANE, KV_RANK)), const((B_HEADS * LANE, LANE)), const((KV_RANK, B_HEADS * B_DV))]
                 + [page(PAGE_SIZE, KV_RANK, g) for g in range(PAGES_PER_STEP)]
                 + [page(B_ROPE, PAGE_SIZE, g) for g in range(PAGES_PER_STEP)],
        out_specs=pl.BlockSpec((None, 1, B_HEADS * B_DV), lambda b, p, pt: (b, 0, 0)),
        scratch_shapes=[pltpu.VMEM((B_HEADS, KV_RANK), BF16), pltpu.VMEM((B_HEADS, LANE), BF16),
                        pltpu.VMEM((B_HEADS, 1), F32), pltpu.VMEM((B_HEADS, 1), F32),
                        pltpu.VMEM((B_HEADS, KV_RANK), F32),
                        pltpu.VMEM((KEYS_PER_STEP, KV_RANK), BF16), pltpu.VMEM((B_ROPE, KEYS_PER_STEP), BF16)],
    )
    return pl.pallas_call(
        _attn_sample_kernel,
        grid_spec=grid_spec,
        out_shape=jax.ShapeDtypeStruct((DEC_BATCH, 1, B_HEADS * B_DV), F32),
        compiler_params=_cparams(("parallel", "arbitrary")),
        name="attn_sample",
    )(page_table, q_s, ckvn_s, sg_s, wk_ext, e_rope, w_uv,
      *([cache_ckv] * PAGES_PER_STEP), *([cache_krope_t] * PAGES_PER_STEP))


def _merge_kernel(x_ref, hn_ref, ob_ref, ga_ref, gb_ref, wpa_ref, wpb_ref, wo_ref, g_ref, b_ref, o_ref):
    y_a = _dot(hn_ref[...], wpa_ref[...])
    y_b = _dot(ob_ref[...], wpb_ref[...])
    mix = jax.nn.sigmoid(ga_ref[...]) * y_a + jax.nn.sigmoid(gb_ref[...]) * y_b
    y = _dot(mix.astype(BF16), wo_ref[...])
    o_ref[...] = _layer_norm(DN_ALPHA * x_ref[...] + y, g_ref[...], b_ref[...])


def _merge(x, hn, ob, ga, gb, layer, w_pa, w_pb, w_o, ln_idx, g, b):
    def row(n):
        return pl.BlockSpec((TM, n), lambda i: (i, 0))

    return pl.pallas_call(
        _merge_kernel,
        grid=(N_TILES,),
        in_specs=[row(D_MODEL), row(A_VW), row(B_HEADS * B_DV), row(D_MODEL), row(D_MODEL),
                  _layer_spec(layer, (A_VW, D_MODEL)), _layer_spec(layer, (B_HEADS * B_DV, D_MODEL)),
                  _layer_spec(layer, (D_MODEL, D_MODEL)), _layer_spec(ln_idx, (1, D_MODEL)),
                  _layer_spec(ln_idx, (1, D_MODEL))],
        out_specs=row(D_MODEL),
        out_shape=jax.ShapeDtypeStruct((T_PAD, D_MODEL), F32),
        compiler_params=_cparams(("parallel",)),
        name="merge_outproj",
    )(x, hn, ob, ga, gb, w_pa, w_pb, w_o, g, b)


def _prep_layer(l, w_in, b_in, q_norm_g, kv_norm_g, w_uq, w_uk, w_uv):
    offs = np.cumsum((0,) + IN_SIZES)

    def grp(a, i):
        return a[..., offs[i]:offs[i + 1]]

    def arrange(a):
        lead = a.shape[:-1]
        z = lambda n: jnp.zeros(lead + (n,), F32)
        sg = jnp.concatenate([grp(a, 6), grp(a, 7), z(ROPE_LANE - 2 * A_HEADS), grp(a, 2),
                              z(LANE - ROPE_LANE - B_ROPE)], axis=-1)
        return jnp.concatenate([grp(a, 0), grp(a, 1), sg, grp(a, 3), grp(a, 4), grp(a, 5), grp(a, 8), grp(a, 9)],
                               axis=-1)

    pad_q = LANE - B_NOPE - B_ROPE
    wuq = jnp.pad(w_uq[l], ((0, 0), (0, 0), (0, pad_q))).reshape(Q_RANK, B_HEADS * LANE)
    wuk = jnp.pad(w_uk[l], ((0, 0), (0, 0), (0, LANE - B_NOPE)))
    place = np.zeros((LANE, B_HEADS, LANE), np.float32)
    for e in range(B_ROPE):
        place[ROPE_LANE + e, :, ROPE_LANE + e] = 1.0
    w_kf = jnp.concatenate([wuk.reshape(KV_RANK, B_HEADS * LANE),
                            jnp.asarray(place).reshape(LANE, B_HEADS * LANE)], axis=0)
    wk_ext = jnp.transpose(wuk, (1, 2, 0)).reshape(B_HEADS * LANE, KV_RANK)
    return {
        "w_in": arrange(w_in[l]).astype(BF16),
        "b_in": arrange(b_in[l])[None, :],
        "qg": q_norm_g[l][None, :],
        "kvg": kv_norm_g[l][None, :],
        "w_uq": wuq.astype(BF16),
        "w_kf": w_kf.astype(BF16),
        "w_uv": w_uv[l].reshape(KV_RANK, B_HEADS * B_DV).astype(BF16),
        "wk_ext": wk_ext.astype(BF16),
    }


def _rope_select():
    e = np.zeros((B_HEADS, LANE, LANE), np.float32)
    for r in range(B_ROPE):
        e[:, ROPE_LANE + r, ROPE_LANE + r] = 1.0
    return jnp.asarray(e.reshape(B_HEADS * LANE, LANE)).astype(BF16)


def _rope_tabs():
    pos_real = N_META + np.arange(SEQ)
    pos_tail = np.zeros((TM,), np.int64)
    pos_tail[:BATCH * N_META] = np.arange(BATCH * N_META) % N_META
    pos_tail[BATCH * N_META:TAIL_ROWS] = PAST_LEN
    pos = jnp.asarray(np.concatenate([pos_real, pos_tail]), dtype=jnp.int32)
    inv = ROPE_BASE ** (-jnp.arange(0, B_ROPE, 2, dtype=F32) / B_ROPE)
    ang = pos.astype(F32)[:, None] * inv[None, :]
    cos, sin = jnp.cos(ang), jnp.sin(ang)
    n = pos.shape[0]
    z = lambda w: jnp.zeros((n, w), F32)
    tail = LANE - ROPE_LANE - B_ROPE
    tc = jnp.concatenate([jnp.ones((n, ROPE_LANE), F32), cos, cos, z(tail)], axis=1)
    ts1 = jnp.concatenate([z(ROPE_LANE), -sin, z(HALF_ROPE), z(tail)], axis=1)
    ts2 = jnp.concatenate([z(ROPE_LANE), z(HALF_ROPE), sin, z(tail)], axis=1)
    return tc, ts1, ts2


def _with_tail(main, meta_rows, samp_rows):
    width = main.shape[1]
    tail = jnp.concatenate([meta_rows.astype(main.dtype), samp_rows.astype(main.dtype),
                            jnp.zeros((TM - TAIL_ROWS, width), main.dtype)], axis=0)
    return lax.dynamic_update_slice(main, tail, (META_OFF, 0))


def kernel(x_prompt, x_sample, cache_ckv, cache_krope, page_table, state_C, state_n, state_m, state_conv, meta,
           w_in, b_in, conv_w, conv_b, mh_g, q_norm_g, kv_norm_g, w_uq, w_uk, w_uv, w_pa, w_pb, w_o,
           ffn1_up, ffn1_down, ffn2_up, ffn2_down, ln_g, ln_b):
    x = jnp.concatenate([x_prompt.reshape(N_REAL, D_MODEL),
                         jnp.tile(meta.astype(F32), (BATCH, 1)),
                         x_sample.reshape(DEC_BATCH, D_MODEL),
                         jnp.zeros((TM - TAIL_ROWS, D_MODEL), F32)], axis=0)
    tabs = _rope_tabs()
    e_rope = _rope_select()
    pt_flat = page_table.reshape(-1)
    samp = slice(SAMP_OFF, SAMP_OFF + DEC_BATCH)
    metas = slice(META_OFF, SAMP_OFF)

    def seq_order(a, width):
        return jnp.concatenate([a[metas].reshape(BATCH, N_META, width), a[:N_REAL].reshape(BATCH, SEQ, width)],
                               axis=1)

    ffn1_up_b, ffn1_down_b = ffn1_up.astype(BF16), ffn1_down.astype(BF16)
    ffn2_up_b, ffn2_down_b = ffn2_up.astype(BF16), ffn2_down.astype(BF16)
    w_pa_b, w_pb_b, w_o_b = w_pa.astype(BF16), w_pb.astype(BF16), w_o.astype(BF16)
    ln_g3 = ln_g.reshape(DEPTH * 3, 1, D_MODEL)
    ln_b3 = ln_b.reshape(DEPTH * 3, 1, D_MODEL)
    state_m4 = state_m[:, :, None, :]
    cache_krope_t = jnp.swapaxes(cache_krope, 2, 3)
    conv_rows = (np.arange(BATCH)[:, None] * SEQ + (SEQ - (CONV_W - 1)) + np.arange(CONV_W - 1)[None, :]).reshape(-1)

    st = {k: [] for k in ("ckv_p", "kr_p", "c_p", "n_p", "m_p", "conv_p", "ckv_s", "kr_s", "c_s", "n_s", "m_s",
                          "conv_s")}
    for l in range(DEPTH):
        wp = _prep_layer(l, w_in, b_in, q_norm_g, kv_norm_g, w_uq, w_uk, w_uv)
        x = _ffn(x, l, ffn1_up_b, ffn1_down_b, 3 * l, ln_g3, ln_b3)
        q, k, vb, ckvn, sg, qk, vm, om, ga, gb = _inproj(x, wp, tabs)

        gates = seq_order(sg[:, :2 * A_HEADS], 2 * A_HEADS)
        gates = gates.reshape(BATCH, S_ALL, 2, A_HEADS)
        gr = jnp.transpose(gates, (0, 3, 2, 1))
        gc = jnp.transpose(gates, (0, 3, 1, 2))
        hn, hn_meta, cx, m_p = _mlstm_prompt(qk, vm, om, gr, gc, conv_w[l], conv_b[l][None], mh_g[l][None])
        hn_s, c_s, n_s, m_s = _mlstm_sample(
            l, qk[samp][:, None, :], state_conv, vm[samp].astype(F32)[:, None, :], om[samp][:, None, :],
            sg[samp][:, None, :], state_m4, state_C, state_n,
            conv_w[l], conv_b[l][None], mh_g[l][None])
        hn = _with_tail(hn, hn_meta, hn_s.reshape(DEC_BATCH, A_VW))

        ob = _attn_prompt(q, k, vb)
        ob_meta = _attn_meta(q, k, vb)
        ob_s = _attn_sample(l, pt_flat, q[samp].astype(F32)[:, None, :], ckvn[samp][:, None, :],
                            sg[samp][:, None, :], wp["wk_ext"], e_rope, wp["w_uv"], cache_ckv, cache_krope_t)
        ob = _with_tail(ob, ob_meta, ob_s.reshape(DEC_BATCH, B_HEADS * B_DV))

        x = _merge(x, hn, ob, ga, gb, l, w_pa_b, w_pb_b, w_o_b, 3 * l + 1, ln_g3, ln_b3)
        x = _ffn(x, l, ffn2_up_b, ffn2_down_b, 3 * l + 2, ln_g3, ln_b3)

        kr_all = sg[:, ROPE_LANE:ROPE_LANE + B_ROPE]
        st["ckv_p"].append(seq_order(ckvn, KV_RANK))
        st["kr_p"].append(seq_order(kr_all, B_ROPE))
        st["c_p"].append(cx[..., :A_DV])
        st["n_p"].append(cx[..., A_DV])
        st["m_p"].append(m_p[:, :, 0, 0])
        st["conv_p"].append(jnp.take(qk, conv_rows, axis=0).reshape(BATCH, CONV_W - 1, 2 * A_QK))
        st["ckv_s"].append(ckvn[samp][:, None, :])
        st["kr_s"].append(kr_all[samp][:, None, :])
        st["c_s"].append(c_s)
        st["n_s"].append(n_s)
        st["m_s"].append(m_s[:, 0, :])
        st["conv_s"].append(jnp.concatenate([state_conv[l][:, 1:], qk[samp][:, None, :]], axis=1))

    y_prompt = x[:N_REAL].reshape(BATCH, SEQ, D_MODEL)
    y_sample = x[samp].reshape(DEC_BATCH, 1, D_MODEL)
    s = {k: jnp.stack(v) for k, v in st.items()}
    return (y_prompt, y_sample, s["ckv_p"], s["kr_p"], s["c_p"], s["n_p"], s["m_p"], s["conv_p"],
            s["ckv_s"], s["kr_s"], s["c_s"], s["n_s"], s["m_s"], s["conv_s"])
```

```python
import functools

import numpy as np
import jax
import jax.numpy as jnp
from jax import lax
from jax.experimental import pallas as pl
from jax.experimental.pallas import tpu as pltpu

F32 = jnp.float32
BF16 = jnp.bfloat16

D_MODEL = 1024
BATCH = 8
SEQ = 2048
DEPTH = 4
DEC_BATCH = 32
PAST_LEN = 16384
PAGE_SIZE = 128
N_PAGES = PAST_LEN // PAGE_SIZE
N_META = 16
A_HEADS = 4
A_DK = 128
A_DV = 128
A_QK = A_HEADS * A_DK
A_VW = A_HEADS * A_DV
CONV_W = 4
B_HEADS = 8
B_NOPE = 64
B_ROPE = 32
B_DV = 64
Q_RANK = 384
KV_RANK = 256
ROPE_BASE = 10000.0
ATTN_SCALE = (B_NOPE + B_ROPE) ** -0.5
LOG2E = 1.4426950408889634
Q_SCALE = ATTN_SCALE * LOG2E
D_FF = 2816
DN_ALPHA = (2 * DEPTH) ** 0.25
LN_EPS = 1e-5
RMS_EPS = 1e-6
IN_SIZES = (Q_RANK, KV_RANK, B_ROPE, 2 * A_QK, A_VW, A_VW, A_HEADS, A_HEADS, D_MODEL, D_MODEL)

LANE = 128
HALF_ROPE = B_ROPE // 2
ROPE_LANE = B_NOPE

TM = 512
N_REAL = BATCH * SEQ
META_OFF = N_REAL
SAMP_OFF = META_OFF + BATCH * N_META
TAIL_ROWS = BATCH * N_META + DEC_BATCH
T_PAD = N_REAL + TM
N_TILES = T_PAD // TM
REAL_TILES_PER_SEQ = SEQ // TM
S_ALL = SEQ + N_META

G_CQ, G_CKV, G_SG, G_QK, G_V, G_O, G_GA, G_GB = 0, 384, 640, 768, 1792, 2304, 2816, 3840
N_IN_P = 4864

FF_CHUNKS = 2
FF_C = D_FF // FF_CHUNKS

MLSTM_L = 256
MLSTM_HP = 2
CONV_PAD = 8

ATT_TQ = 512
ATT_TK = 512
PAGES_PER_STEP = 32
KEYS_PER_STEP = PAGES_PER_STEP * PAGE_SIZE

VMEM_LIMIT = 56 * 1024 * 1024


def _cparams(sem):
    return pltpu.CompilerParams(dimension_semantics=sem, vmem_limit_bytes=VMEM_LIMIT)


def _const_spec(shape):
    nd = len(shape)
    return pl.BlockSpec(shape, lambda *_: (0,) * nd, pipeline_mode=pl.Buffered(1))


def _layer_spec(layer, shape):
    nd = len(shape)
    return pl.BlockSpec((None,) + shape, lambda *_: (layer,) + (0,) * nd, pipeline_mode=pl.Buffered(1))


def _layer_norm(z, g, b):
    mu = jnp.mean(z, axis=-1, keepdims=True)
    zc = z - mu
    var = jnp.mean(zc * zc, axis=-1, keepdims=True)
    return zc * lax.rsqrt(var + LN_EPS) * g + b


def _rms_norm(z, g):
    return z * lax.rsqrt(jnp.mean(z * z, axis=-1, keepdims=True) + RMS_EPS) * g


def _dot(a, b):
    return jnp.dot(a, b, preferred_element_type=F32)


def _dot_nt(a, b):
    return lax.dot_general(a, b, (((1,), (1,)), ((), ())), preferred_element_type=F32)


def _dot_tn(a, b):
    return lax.dot_general(a, b, (((0,), (0,)), ((), ())), preferred_element_type=F32)


def _log_sigmoid(x):
    return jnp.minimum(x, 0.0) - jnp.log1p(jnp.exp(-jnp.abs(x)))


def _swiglu_postnorm(x, wup_ref, wdn_ref, g, b):
    xb = x.astype(BF16)
    y = jnp.zeros_like(x)
    for c in range(FF_CHUNKS):
        lo = c * FF_C
        a = _dot(xb, wup_ref[:, lo:lo + FF_C])
        u = _dot(xb, wup_ref[:, D_FF + lo:D_FF + lo + FF_C])
        h = (a * jax.nn.sigmoid(a) * u).astype(BF16)
        y = y + _dot(h, wdn_ref[lo:lo + FF_C, :])
    return _layer_norm(DN_ALPHA * x + 0.5 * y, g, b)


def _ffn_kernel(x_ref, wup_ref, wdn_ref, g_ref, b_ref, o_ref):
    o_ref[...] = _swiglu_postnorm(x_ref[...], wup_ref, wdn_ref, g_ref[...], b_ref[...])


def _ffn(x, layer, w_up, w_down, ln_idx, g, b):
    row = pl.BlockSpec((TM, D_MODEL), lambda i: (i, 0))
    return pl.pallas_call(
        _ffn_kernel,
        grid=(N_TILES,),
        in_specs=[row, _layer_spec(layer, (D_MODEL, 2 * D_FF)), _layer_spec(layer, (D_FF, D_MODEL)),
                  _layer_spec(ln_idx, (1, D_MODEL)), _layer_spec(ln_idx, (1, D_MODEL))],
        out_specs=row,
        out_shape=jax.ShapeDtypeStruct((T_PAD, D_MODEL), F32),
        compiler_params=_cparams(("parallel",)),
        name="ffn_postnorm",
    )(x, w_up, w_down, g, b)


def _rope(z, tc, ts1, ts2):
    return z * tc + pltpu.roll(z, LANE - HALF_ROPE, 1) * ts1 + pltpu.roll(z, HALF_ROPE, 1) * ts2


def _inproj_kernel(x_ref, w_ref, b_ref, qg_ref, kvg_ref, wuq_ref, wkf_ref, wuv_ref, tc_ref, ts1_ref, ts2_ref,
                   q_ref, k_ref, vb_ref, ckv_ref, sg_ref, qk_ref, vm_ref, om_ref, ga_ref, gb_ref):
    xb = x_ref[...].astype(BF16)

    def proj(off, n):
        return _dot(xb, w_ref[:, off:off + n]) + b_ref[:, off:off + n]

    tc, ts1, ts2 = tc_ref[...], ts1_ref[...], ts2_ref[...]
    cqn = _rms_norm(proj(G_CQ, Q_RANK), qg_ref[...]).astype(BF16)
    q = _dot(cqn, wuq_ref[...])
    for h in range(B_HEADS):
        qh = _rope(q[:, h * LANE:(h + 1) * LANE], tc, ts1, ts2)
        q_ref[:, h * LANE:(h + 1) * LANE] = (qh * Q_SCALE).astype(BF16)
    ckvn = _rms_norm(proj(G_CKV, KV_RANK), kvg_ref[...])
    ckv_ref[...] = ckvn
    sg = _rope(proj(G_SG, LANE), tc, ts1, ts2)
    lane = lax.broadcasted_iota(jnp.int32, sg.shape, 1)
    sg = jnp.where((lane >= A_HEADS) & (lane < 2 * A_HEADS), _log_sigmoid(sg), sg)
    sg_ref[...] = sg
    kin = jnp.concatenate([ckvn.astype(BF16), sg.astype(BF16)], axis=1)
    k_ref[...] = _dot(kin, wkf_ref[...]).astype(BF16)
    vb_ref[...] = _dot(kin[:, :KV_RANK], wuv_ref[...]).astype(BF16)
    qk_ref[...] = proj(G_QK, 2 * A_QK)
    vm_ref[...] = proj(G_V, A_VW).astype(BF16)
    om_ref[...] = proj(G_O, A_VW)
    ga_ref[...] = jax.nn.sigmoid(proj(G_GA, D_MODEL)).astype(BF16)
    gb_ref[...] = jax.nn.sigmoid(proj(G_GB, D_MODEL)).astype(BF16)


def _inproj(x, wp, tabs):
    def row(n):
        return pl.BlockSpec((TM, n), lambda i: (i, 0))

    tab = pl.BlockSpec((TM, LANE), lambda i: (jnp.where(i < N_REAL // TM, i % REAL_TILES_PER_SEQ,
                                                       REAL_TILES_PER_SEQ), 0))
    widths = (B_HEADS * LANE, B_HEADS * LANE, B_HEADS * B_DV, KV_RANK, LANE, 2 * A_QK, A_VW, A_VW, D_MODEL, D_MODEL)
    dtypes = (BF16, BF16, BF16, F32, F32, F32, BF16, F32, BF16, BF16)
    return pl.pallas_call(
        _inproj_kernel,
        grid=(N_TILES,),
        in_specs=[row(D_MODEL), _const_spec((D_MODEL, N_IN_P)), _const_spec((1, N_IN_P)),
                  _const_spec((1, Q_RANK)), _const_spec((1, KV_RANK)),
                  _const_spec((Q_RANK, B_HEADS * LANE)), _const_spec((KV_RANK + LANE, B_HEADS * LANE)),
                  _const_spec((KV_RANK, B_HEADS * B_DV)), tab, tab, tab],
        out_specs=[row(n) for n in widths],
        out_shape=[jax.ShapeDtypeStruct((T_PAD, n), d) for n, d in zip(widths, dtypes)],
        compiler_params=_cparams(("parallel",)),
        name="in_proj",
    )(x, wp["w_in"], wp["b_in"], wp["qg"], wp["kvg"], wp["w_uq"], wp["w_kf"], wp["w_uv"], *tabs)


def _mlstm_prompt_kernel(qm_ref, km_ref, q0_ref, k0_ref, vm_ref, v0_ref, om_ref, o0_ref, gr_ref, gc_ref,
                         cwq_ref, cwk_ref, cbq_ref, cbk_ref, mhg_ref,
                         hm_ref, h0_ref, cx_ref, m_ref, uq_ref, uk_ref, vx_ref):
    for u_ref, a0_ref, am_ref in ((uq_ref, q0_ref, qm_ref), (uk_ref, k0_ref, km_ref)):
        u_ref[0:CONV_PAD, :] = jnp.zeros((CONV_PAD, MLSTM_HP * LANE), F32)
        u_ref[CONV_PAD:CONV_PAD + N_META, :] = a0_ref[...]
        u_ref[CONV_PAD + N_META:, :] = am_ref[...]
    for hh in range(MLSTM_HP):
        lanes = slice(hh * LANE, (hh + 1) * LANE)
        vx_ref[hh, 0:N_META, 0:LANE] = v0_ref[:, lanes]
        vx_ref[hh, N_META:, 0:LANE] = vm_ref[:, lanes]
        vx_ref[hh, :, LANE:] = jnp.ones((S_ALL, LANE), BF16)

    def conv_silu(u_ref, w_ref, b_ref, t0, length):
        acc = b_ref[...]
        for j in range(CONV_W):
            lo = CONV_PAD - (CONV_W - 1) + j + t0
            acc = acc + w_ref[j:j + 1, :] * u_ref[lo:lo + length, :]
        return acc * jax.nn.sigmoid(acc)

    def head_chunk(hh, q, k, t0, length, o_ref, h_ref, r0, cx, m):
        lanes = slice(hh * LANE, (hh + 1) * LANE)
        qb, kb = q.astype(BF16), k.astype(BF16)
        ig_r = gr_ref[hh, 0:1, t0:t0 + length]
        lf_r = gr_ref[hh, 1:2, t0:t0 + length]
        ig_c = gc_ref[hh, t0:t0 + length, 0:1]
        lf_c = gc_ref[hh, t0:t0 + length, 1:2]
        row = lax.broadcasted_iota(jnp.int32, (length, length), 0)
        col = lax.broadcasted_iota(jnp.int32, (length, length), 1)
        tri = row >= col
        b_c = jnp.sum(jnp.where(tri, lf_r, 0.0), axis=1, keepdims=True)
        b_r = jnp.sum(jnp.where(row <= col, lf_c, 0.0), axis=0, keepdims=True)
        dmat = jnp.where(tri, b_c - b_r + ig_r, -jnp.inf)
        inter = b_c + m
        m_row = jnp.maximum(inter, jnp.max(dmat, axis=1, keepdims=True))
        s = _dot_nt(qb, kb) * jnp.exp(dmat - m_row)
        w_prev = jnp.exp(inter - m_row)
        vx = vx_ref[hh, t0:t0 + length, :]
        tot = w_prev * _dot(qb, cx.astype(BF16)) + _dot(s.astype(BF16), vx)
        h = tot[:, :A_DV] / jnp.maximum(jnp.abs(tot[:, A_DV:]), jnp.exp(-m_row))
        b_last = b_c[length - 1:length, :]
        g = b_last - b_c + ig_c
        m_new = jnp.maximum(b_last + m, jnp.max(g, axis=0, keepdims=True))
        decay = jnp.exp(b_last + m - m_new)
        wk = (jnp.exp(g - m_new) * k).astype(BF16)
        cx_new = decay * cx + _dot_tn(wk, vx)
        mu = jnp.mean(h, axis=-1, keepdims=True)
        hc = h - mu
        var = jnp.mean(hc * hc, axis=-1, keepdims=True)
        hn = hc * lax.rsqrt(var + LN_EPS) * mhg_ref[:, lanes] * jax.nn.sigmoid(o_ref[r0:r0 + length, lanes])
        h_ref[r0:r0 + length, lanes] = hn.astype(BF16)
        return cx_new, m_new

    def chunk(t0, length, o_ref, h_ref, r0, states):
        q2 = conv_silu(uq_ref, cwq_ref, cbq_ref, t0, length)
        k2 = conv_silu(uk_ref, cwk_ref, cbk_ref, t0, length) * (A_DK ** -0.5)
        return [head_chunk(hh, q2[:, hh * LANE:(hh + 1) * LANE], k2[:, hh * LANE:(hh + 1) * LANE],
                           t0, length, o_ref, h_ref, r0, *states[hh]) for hh in range(MLSTM_HP)]

    states = [(jnp.zeros((A_DK, 2 * LANE), F32), jnp.zeros((1, 1), F32)) for _ in range(MLSTM_HP)]
    states = chunk(0, N_META, o0_ref, h0_ref, 0, states)
    for c in range(SEQ // MLSTM_L):
        states = chunk(N_META + c * MLSTM_L, MLSTM_L, om_ref, hm_ref, c * MLSTM_L, states)
    for hh, (cx, m) in enumerate(states):
        cx_ref[hh] = cx
        m_ref[hh] = jnp.broadcast_to(m, (1, LANE))


def _mlstm_prompt(qk, vm, om, gr, gc, conv_w, conv_b, mh_g):
    meta_blk = META_OFF // N_META
    width = MLSTM_HP * LANE
    k_off = A_HEADS // MLSTM_HP

    def main(off):
        return pl.BlockSpec((SEQ, width), lambda b, j: (b, j + off))

    def meta(off):
        return pl.BlockSpec((N_META, width), lambda b, j: (meta_blk + b, j + off))

    def wcol(rows, off):
        return pl.BlockSpec((rows, width), lambda b, j: (0, j + off))

    def per_head(*shape):
        nd = len(shape)
        return pl.BlockSpec((None, MLSTM_HP) + shape, lambda b, j: (b, j) + (0,) * nd)

    return pl.pallas_call(
        _mlstm_prompt_kernel,
        grid=(BATCH, A_HEADS // MLSTM_HP),
        in_specs=[main(0), main(k_off), meta(0), meta(k_off), main(0), meta(0), main(0), meta(0),
                  per_head(2, S_ALL), per_head(S_ALL, 2),
                  wcol(CONV_W, 0), wcol(CONV_W, k_off), wcol(1, 0), wcol(1, k_off), wcol(1, 0)],
        out_specs=[main(0),
                   pl.BlockSpec((N_META, width), lambda b, j: (b, j)),
                   per_head(A_DK, 2 * LANE), per_head(1, LANE)],
        out_shape=[jax.ShapeDtypeStruct((N_REAL, A_VW), BF16),
                   jax.ShapeDtypeStruct((BATCH * N_META, A_VW), BF16),
                   jax.ShapeDtypeStruct((BATCH, A_HEADS, A_DK, 2 * LANE), F32),
                   jax.ShapeDtypeStruct((BATCH, A_HEADS, 1, LANE), F32)],
        scratch_shapes=[pltpu.VMEM((CONV_PAD + S_ALL, width), F32), pltpu.VMEM((CONV_PAD + S_ALL, width), F32),
                        pltpu.VMEM((MLSTM_HP, S_ALL, 2 * LANE), BF16)],
        compiler_params=_cparams(("parallel", "parallel")),
        name="mlstm_prompt",
    )(qk, qk, qk, qk, vm, vm, om, om, gr, gc, conv_w, conv_w, conv_b, conv_b, mh_g)


def mlstm_prompt_from_rows(qk, vm, om, gates, conv_w, conv_b, mh_g):
    g = jnp.concatenate([gates[META_OFF:SAMP_OFF].reshape(BATCH, N_META, 2, A_HEADS),
                         gates[:N_REAL].reshape(BATCH, SEQ, 2, A_HEADS)], axis=1)
    gr = jnp.transpose(g, (0, 3, 2, 1))
    gc = jnp.transpose(g, (0, 3, 1, 2))
    hn, hn_meta, cx, m = _mlstm_prompt(qk, vm, om, gr, gc, conv_w, conv_b[None], mh_g[None])
    return hn, hn_meta, cx[..., :A_DV], cx[..., A_DV], m[:, :, 0, 0]


def _mlstm_sample_kernel(qk_ref, cprev_ref, v_ref, o_ref, sg_ref, m_ref, c_ref, n_ref, cw_ref, cb_ref, mhg_ref,
                         hn_ref, cnew_ref, nnew_ref, mnew_ref):
    acc = (cb_ref[...] + jnp.sum(cw_ref[0:CONV_W - 1, :] * cprev_ref[...], axis=0, keepdims=True)
           + cw_ref[CONV_W - 1:CONV_W, :] * qk_ref[...])
    a = acc * jax.nn.sigmoid(acc)
    sg = sg_ref[...]
    m_in = m_ref[...]
    row8 = lax.broadcasted_iota(jnp.int32, (8, LANE), 0)
    for h in range(A_HEADS):
        q = a[:, h * A_DK:(h + 1) * A_DK]
        k = a[:, A_QK + h * A_DK:A_QK + (h + 1) * A_DK] * (A_DK ** -0.5)
        v = v_ref[:, h * A_DV:(h + 1) * A_DV]
        ig = sg[:, h:h + 1]
        lf = sg[:, A_HEADS + h:A_HEADS + h + 1]
        m = m_in[:, h:h + 1]
        c = c_ref[h]
        n = n_ref[h:h + 1, :]
        inter = lf + m
        m_row = jnp.maximum(inter, ig)
        s = jnp.sum(q * k, axis=-1, keepdims=True) * jnp.exp(ig - m_row)
        w_prev = jnp.exp(inter - m_row)
        q8 = jnp.broadcast_to(q, (8, A_DK)).astype(BF16)
        qc = _dot(q8, c.astype(BF16))[0:1, :]
        num = w_prev * qc + s * v
        den = w_prev * jnp.sum(q * n, axis=-1, keepdims=True) + s
        hh = num / jnp.maximum(jnp.abs(den), jnp.exp(-m_row))
        wk = jnp.exp(ig - m_row) * k
        wk8 = jnp.where(row8 == 0, jnp.broadcast_to(wk, (8, A_DK)), 0.0).astype(BF16)
        v8 = jnp.broadcast_to(v, (8, A_DV)).astype(BF16)
        cnew_ref[h] = w_prev * c + _dot_tn(wk8, v8)
        nnew_ref[h:h + 1, :] = w_prev * n + wk
        mnew_ref[:, h:h + 1] = m_row
        mu = jnp.mean(hh, axis=-1, keepdims=True)
        hc = hh - mu
        var = jnp.mean(hc * hc, axis=-1, keepdims=True)
        hn = hc * lax.rsqrt(var + LN_EPS) * mhg_ref[:, h * A_DV:(h + 1) * A_DV]
        hn_ref[:, h * A_DV:(h + 1) * A_DV] = hn * jax.nn.sigmoid(o_ref[:, h * A_DV:(h + 1) * A_DV])


def _mlstm_sample(layer, qk_s, conv_prev, v_s, o_s, sg_s, m_s, state_c, state_n, conv_w, conv_b, mh_g):
    def per_b(*shape):
        nd = len(shape)
        return pl.BlockSpec((None,) + shape, lambda b: (b,) + (0,) * nd)

    def per_lb(*shape):
        nd = len(shape)
        return pl.BlockSpec((None, None) + shape, lambda b: (layer, b) + (0,) * nd)

    return pl.pallas_call(
        _mlstm_sample_kernel,
        grid=(DEC_BATCH,),
        in_specs=[per_b(1, 2 * A_QK), per_lb(CONV_W - 1, 2 * A_QK), per_b(1, A_VW), per_b(1, A_VW), per_b(1, LANE),
                  per_lb(1, A_HEADS), per_lb(A_HEADS, A_DK, A_DV), per_lb(A_HEADS, A_DK),
                  _const_spec((CONV_W, 2 * A_QK)), _const_spec((1, 2 * A_QK)), _const_spec((1, A_VW))],
        out_specs=[per_b(1, A_VW), per_b(A_HEADS, A_DK, A_DV), per_b(A_HEADS, A_DK), per_b(1, A_HEADS)],
        out_shape=[jax.ShapeDtypeStruct((DEC_BATCH, 1, A_VW), F32),
                   jax.ShapeDtypeStruct((DEC_BATCH, A_HEADS, A_DK, A_DV), F32),
                   jax.ShapeDtypeStruct((DEC_BATCH, A_HEADS, A_DK), F32),
                   jax.ShapeDtypeStruct((DEC_BATCH, 1, A_HEADS), F32)],
        compiler_params=_cparams(("parallel",)),
        name="mlstm_sample",
    )(qk_s, conv_prev, v_s, o_s, sg_s, m_s, state_c, state_n, conv_w, conv_b, mh_g)


def _attn_prompt_kernel(q_ref, k_ref, v_ref, k0_ref, v0_ref, o_ref):
    i = pl.program_id(2)
    sub = ATT_TQ // ATT_TK
    head_lanes = [slice(hh * LANE, (hh + 1) * LANE) for hh in range(2)]

    def head_tile(lanes, state, r0, diag):
        m, l, acc = state
        s = _dot_nt(q_ref[:, lanes], k_ref[pl.ds(r0, ATT_TK), lanes])
        if diag is not None:
            row = lax.broadcasted_iota(jnp.int32, (ATT_TQ, ATT_TK), 0)
            col = lax.broadcasted_iota(jnp.int32, (ATT_TQ, ATT_TK), 1)
            s = jnp.where(col + diag * ATT_TK <= row, s, -jnp.inf)
        m_new = jnp.maximum(m, jnp.max(s, axis=-1, keepdims=True))
        alpha = jnp.exp2(m - m_new)
        p = jnp.exp2(s - m_new)
        l = alpha * l + jnp.sum(p, axis=-1, keepdims=True)
        acc = alpha * acc + _dot(p.astype(BF16), v_ref[pl.ds(r0, ATT_TK), :])
        return m_new, l, acc

    def kv_tile(kt, carry, diag):
        r0 = pl.multiple_of(kt * ATT_TK, ATT_TK)
        return tuple(head_tile(lanes, st, r0, diag) for lanes, st in zip(head_lanes, carry))

    init = []
    for lanes in head_lanes:
        s0 = _dot_nt(q_ref[:, lanes], k0_ref[:, lanes])
        m = jnp.max(s0, axis=-1, keepdims=True)
        p0 = jnp.exp2(s0 - m)
        init.append((m, jnp.sum(p0, axis=-1, keepdims=True), _dot(p0.astype(BF16), v0_ref[...])))
    carry = lax.fori_loop(0, i * sub, functools.partial(kv_tile, diag=None), tuple(init))
    for d in range(sub):
        carry = kv_tile(i * sub + d, carry, d)
    outs = [acc / l for _, l, acc in carry]
    lane = lax.broadcasted_iota(jnp.int32, (ATT_TQ, LANE), 1)
    o_ref[...] = jnp.where(lane < B_DV, outs[0], outs[1]).astype(BF16)


def _attn_prompt(q, k, vb):
    assert ATT_TQ % ATT_TK == 0
    nq = SEQ // ATT_TQ
    meta_blk = META_OFF // N_META
    return pl.pallas_call(
        _attn_prompt_kernel,
        grid=(BATCH, B_HEADS // 2, nq),
        in_specs=[pl.BlockSpec((ATT_TQ, 2 * LANE), lambda b, j, i: (b * nq + i, j)),
                  pl.BlockSpec((SEQ, 2 * LANE), lambda b, j, i: (b, j)),
                  pl.BlockSpec((SEQ, LANE), lambda b, j, i: (b, j)),
                  pl.BlockSpec((N_META, 2 * LANE), lambda b, j, i: (meta_blk + b, j)),
                  pl.BlockSpec((N_META, LANE), lambda b, j, i: (meta_blk + b, j))],
        out_specs=pl.BlockSpec((ATT_TQ, LANE), lambda b, j, i: (b * nq + i, j)),
        out_shape=jax.ShapeDtypeStruct((N_REAL, B_HEADS * B_DV), BF16),
        compiler_params=_cparams(("parallel", "parallel", "parallel")),
        name="attn_prompt",
    )(q, k, vb, k, vb)


def _attn_meta_kernel(q_ref, k_ref, v_ref, o_ref):
    row = lax.broadcasted_iota(jnp.int32, (N_META, N_META), 0)
    col = lax.broadcasted_iota(jnp.int32, (N_META, N_META), 1)
    lane = lax.broadcasted_iota(jnp.int32, (N_META, B_HEADS * B_DV), 1)
    out = jnp.zeros((N_META, B_HEADS * B_DV), F32)
    for h in range(B_HEADS):
        lanes = slice(h * LANE, (h + 1) * LANE)
        s = _dot_nt(q_ref[:, lanes], k_ref[:, lanes])
        s = jnp.where(col <= row, s, -jnp.inf)
        p = jnp.exp2(s - jnp.max(s, axis=-1, keepdims=True))
        l = jnp.sum(p, axis=-1, keepdims=True)
        o = _dot(p.astype(BF16), v_ref[...]) / l
        out = jnp.where(lane // B_DV == h, o, out)
    o_ref[...] = out.astype(BF16)


def _attn_meta(q, k, vb):
    meta_blk = META_OFF // N_META

    def spec(n):
        return pl.BlockSpec((N_META, n), lambda b: (meta_blk + b, 0))

    return pl.pallas_call(
        _attn_meta_kernel,
        grid=(BATCH,),
        in_specs=[spec(B_HEADS * LANE), spec(B_HEADS * LANE), spec(B_HEADS * B_DV)],
        out_specs=pl.BlockSpec((N_META, B_HEADS * B_DV), lambda b: (b, 0)),
        out_shape=jax.ShapeDtypeStruct((BATCH * N_META, B_HEADS * B_DV), BF16),
        compiler_params=_cparams(("parallel",)),
        name="attn_meta",
    )(q, k, vb)


def _attn_sample_kernel(pt_ref, q_ref, ckvn_ref, sg_ref, wk_ref, er_ref, wuv_ref, *rest):
    ck_refs = rest[:PAGES_PER_STEP]
    kr_refs = rest[PAGES_PER_STEP:2 * PAGES_PER_STEP]
    o_ref, qlat_ref, qr_ref, m_ref, l_ref, acc_ref, kb_ref, krb_ref = rest[2 * PAGES_PER_STEP:]
    p_idx = pl.program_id(1)

    @pl.when(p_idx == 0)
    def _init():
        q = q_ref[...]
        row = lax.broadcasted_iota(jnp.int32, (B_HEADS, B_HEADS * LANE), 0)
        lane = lax.broadcasted_iota(jnp.int32, (B_HEADS, B_HEADS * LANE), 1)
        qbd = jnp.where(lane // LANE == row, jnp.broadcast_to(q, (B_HEADS, B_HEADS * LANE)), 0.0).astype(BF16)
        qlat = _dot(qbd, wk_ref[...])
        qr = _dot(qbd, er_ref[...])
        qlat_ref[...] = qlat.astype(BF16)
        qr_ref[...] = qr.astype(BF16)
        ckvn = ckvn_ref[...]
        s_new = (jnp.sum(qlat * ckvn, axis=-1, keepdims=True)
                 + jnp.sum(qr * sg_ref[...], axis=-1, keepdims=True))
        m_ref[...] = s_new
        l_ref[...] = jnp.ones_like(s_new)
        acc_ref[...] = jnp.broadcast_to(ckvn, (B_HEADS, KV_RANK))

    for g in range(PAGES_PER_STEP):
        kb_ref[g * PAGE_SIZE:(g + 1) * PAGE_SIZE, :] = ck_refs[g][...].astype(BF16)
        krb_ref[:, g * PAGE_SIZE:(g + 1) * PAGE_SIZE] = kr_refs[g][...].astype(BF16)
    qr = qr_ref[:, ROPE_LANE:ROPE_LANE + B_ROPE]
    s = _dot_nt(qlat_ref[...], kb_ref[...]) + _dot(qr, krb_ref[...])
    m_old = m_ref[...]
    m_new = jnp.maximum(m_old, jnp.max(s, axis=-1, keepdims=True))
    alpha = jnp.exp2(m_old - m_new)
    p = jnp.exp2(s - m_new)
    l_ref[...] = alpha * l_ref[...] + jnp.sum(p, axis=-1, keepdims=True)
    acc_ref[...] = alpha * acc_ref[...] + _dot(p.astype(BF16), kb_ref[...])
    m_ref[...] = m_new

    @pl.when(p_idx == pl.num_programs(1) - 1)
    def _fin():
        o_lat = (acc_ref[...] / l_ref[...]).astype(BF16)
        o_all = _dot(o_lat, wuv_ref[...])
        row = lax.broadcasted_iota(jnp.int32, o_all.shape, 0)
        lane = lax.broadcasted_iota(jnp.int32, o_all.shape, 1)
        o_ref[...] = jnp.sum(jnp.where(lane // B_DV == row, o_all, 0.0), axis=0, keepdims=True)


def _attn_sample(layer, page_table, q_s, ckvn_s, sg_s, wk_ext, e_rope, w_uv, cache_ckv, cache_krope_t):
    steps = N_PAGES // PAGES_PER_STEP

    def per_b(n):
        return pl.BlockSpec((None, 1, n), lambda b, p, pt: (b, 0, 0))

    def const(shape):
        return pl.BlockSpec(shape, lambda b, p, pt: (0, 0), pipeline_mode=pl.Buffered(1))

    def page(rows, width, g):
        return pl.BlockSpec((None, None, rows, width),
                            lambda b, p, pt: (layer, pt[b * N_PAGES + p * PAGES_PER_STEP + g], 0, 0))

    grid_spec = pltpu.PrefetchScalarGridSpec(
        num_scalar_prefetch=1,
        grid=(DEC_BATCH, steps),
        in_specs=[per_b(B_HEADS * LANE), per_b(KV_RANK), per_b(LANE),
                  const((B_HEADS * LANE, KV_RANK)), const((B_HEADS * LANE, LANE)), const((KV_RANK, B_HEADS * B_DV))]
                 + [page(PAGE_SIZE, KV_RANK, g) for g in range(PAGES_PER_STEP)]
                 + [page(B_ROPE, PAGE_SIZE, g) for g in range(PAGES_PER_STEP)],
        out_specs=pl.BlockSpec((None, 1, B_HEADS * B_DV), lambda b, p, pt: (b, 0, 0)),
        scratch_shapes=[pltpu.VMEM((B_HEADS, KV_RANK), BF16), pltpu.VMEM((B_HEADS, LANE), BF16),
                        pltpu.VMEM((B_HEADS, 1), F32), pltpu.VMEM((B_HEADS, 1), F32),
                        pltpu.VMEM((B_HEADS, KV_RANK), F32),
                        pltpu.VMEM((KEYS_PER_STEP, KV_RANK), BF16), pltpu.VMEM((B_ROPE, KEYS_PER_STEP), BF16)],
    )
    return pl.pallas_call(
        _attn_sample_kernel,
        grid_spec=grid_spec,
        out_shape=jax.ShapeDtypeStruct((DEC_BATCH, 1, B_HEADS * B_DV), F32),
        compiler_params=_cparams(("parallel", "arbitrary")),
        name="attn_sample",
    )(page_table, q_s, ckvn_s, sg_s, wk_ext, e_rope, w_uv,
      *([cache_ckv] * PAGES_PER_STEP), *([cache_krope_t] * PAGES_PER_STEP))


def _merge_ffn_kernel(x_ref, hnm_ref, hnt_ref, obm_ref, obt_ref, ga_ref, gb_ref, wpa_ref, wpb_ref, wo_ref,
                      g1_ref, b1_ref, wup_ref, wdn_ref, g2_ref, b2_ref, o_ref):
    is_tail = pl.program_id(0) == N_TILES - 1
    hn = jnp.where(is_tail, hnt_ref[...], hnm_ref[...])
    ob = jnp.where(is_tail, obt_ref[...], obm_ref[...])
    y_a = _dot(hn, wpa_ref[...])
    y_b = _dot(ob, wpb_ref[...])
    mix = ga_ref[...].astype(F32) * y_a + gb_ref[...].astype(F32) * y_b
    y = _dot(mix.astype(BF16), wo_ref[...])
    x = _layer_norm(DN_ALPHA * x_ref[...] + y, g1_ref[...], b1_ref[...])
    o_ref[...] = _swiglu_postnorm(x, wup_ref, wdn_ref, g2_ref[...], b2_ref[...])


def _merge_ffn(x, hn, hn_tail, ob, ob_tail, ga, gb, layer, w_pa, w_pb, w_o, w_up, w_down, ln_idx, g, b):
    def row(n):
        return pl.BlockSpec((TM, n), lambda i: (i, 0))

    def main(n):
        return pl.BlockSpec((TM, n), lambda i: (jnp.minimum(i, N_REAL // TM - 1), 0))

    def tail(n):
        return pl.BlockSpec((TM, n), lambda i: (0, 0))

    return pl.pallas_call(
        _merge_ffn_kernel,
        grid=(N_TILES,),
        in_specs=[row(D_MODEL), main(A_VW), tail(A_VW), main(B_HEADS * B_DV), tail(B_HEADS * B_DV),
                  row(D_MODEL), row(D_MODEL),
                  _layer_spec(layer, (A_VW, D_MODEL)), _layer_spec(layer, (B_HEADS * B_DV, D_MODEL)),
                  _layer_spec(layer, (D_MODEL, D_MODEL)), _layer_spec(ln_idx, (1, D_MODEL)),
                  _layer_spec(ln_idx, (1, D_MODEL)),
                  _layer_spec(layer, (D_MODEL, 2 * D_FF)), _layer_spec(layer, (D_FF, D_MODEL)),
                  _layer_spec(ln_idx + 1, (1, D_MODEL)), _layer_spec(ln_idx + 1, (1, D_MODEL))],
        out_specs=row(D_MODEL),
        out_shape=jax.ShapeDtypeStruct((T_PAD, D_MODEL), F32),
        compiler_params=_cparams(("parallel",)),
        name="merge_ffn",
    )(x, hn, hn_tail, ob, ob_tail, ga, gb, w_pa, w_pb, w_o, g, b, w_up, w_down, g, b)


def _prep_layer(l, w_in, b_in, q_norm_g, kv_norm_g, w_uq, w_uk, w_uv):
    offs = np.cumsum((0,) + IN_SIZES)

    def grp(a, i):
        return a[..., offs[i]:offs[i + 1]]

    def arrange(a):
        lead = a.shape[:-1]
        z = lambda n: jnp.zeros(lead + (n,), F32)
        sg = jnp.concatenate([grp(a, 6), grp(a, 7), z(ROPE_LANE - 2 * A_HEADS), grp(a, 2),
                              z(LANE - ROPE_LANE - B_ROPE)], axis=-1)
        return jnp.concatenate([grp(a, 0), grp(a, 1), sg, grp(a, 3), grp(a, 4), grp(a, 5), grp(a, 8), grp(a, 9)],
                               axis=-1)

    pad_q = LANE - B_NOPE - B_ROPE
    wuq = jnp.pad(w_uq[l], ((0, 0), (0, 0), (0, pad_q))).reshape(Q_RANK, B_HEADS * LANE)
    wuk = jnp.pad(w_uk[l], ((0, 0), (0, 0), (0, LANE - B_NOPE)))
    place = np.zeros((LANE, B_HEADS, LANE), np.float32)
    for e in range(B_ROPE):
        place[ROPE_LANE + e, :, ROPE_LANE + e] = 1.0
    w_kf = jnp.concatenate([wuk.reshape(KV_RANK, B_HEADS * LANE),
                            jnp.asarray(place).reshape(LANE, B_HEADS * LANE)], axis=0)
    wk_ext = jnp.transpose(wuk, (1, 2, 0)).reshape(B_HEADS * LANE, KV_RANK)
    return {
        "w_in": arrange(w_in[l]).astype(BF16),
        "b_in": arrange(b_in[l])[None, :],
        "qg": q_norm_g[l][None, :],
        "kvg": kv_norm_g[l][None, :],
        "w_uq": wuq.astype(BF16),
        "w_kf": w_kf.astype(BF16),
        "w_uv": w_uv[l].reshape(KV_RANK, B_HEADS * B_DV).astype(BF16),
        "wk_ext": wk_ext.astype(BF16),
    }


def _rope_select():
    e = np.zeros((B_HEADS, LANE, LANE), np.float32)
    for r in range(B_ROPE):
        e[:, ROPE_LANE + r, ROPE_LANE + r] = 1.0
    return jnp.asarray(e.reshape(B_HEADS * LANE, LANE)).astype(BF16)


def _rope_tabs():
    pos_real = N_META + np.arange(SEQ)
    pos_tail = np.zeros((TM,), np.int64)
    pos_tail[:BATCH * N_META] = np.arange(BATCH * N_META) % N_META
    pos_tail[BATCH * N_META:TAIL_ROWS] = PAST_LEN
    pos = jnp.asarray(np.concatenate([pos_real, pos_tail]), dtype=jnp.int32)
    inv = ROPE_BASE ** (-jnp.arange(0, B_ROPE, 2, dtype=F32) / B_ROPE)
    ang = pos.astype(F32)[:, None] * inv[None, :]
    cos, sin = jnp.cos(ang), jnp.sin(ang)
    n = pos.shape[0]
    z = lambda w: jnp.zeros((n, w), F32)
    tail = LANE - ROPE_LANE - B_ROPE
    tc = jnp.concatenate([jnp.ones((n, ROPE_LANE), F32), cos, cos, z(tail)], axis=1)
    ts1 = jnp.concatenate([z(ROPE_LANE), -sin, z(HALF_ROPE), z(tail)], axis=1)
    ts2 = jnp.concatenate([z(ROPE_LANE), z(HALF_ROPE), sin, z(tail)], axis=1)
    return tc, ts1, ts2


def _tail_tile(meta_rows, samp_rows):
    width = meta_rows.shape[1]
    return jnp.concatenate([meta_rows, samp_rows.astype(meta_rows.dtype),
                            jnp.zeros((TM - TAIL_ROWS, width), meta_rows.dtype)], axis=0)


def kernel(x_prompt, x_sample, cache_ckv, cache_krope, page_table, state_C, state_n, state_m, state_conv, meta,
           w_in, b_in, conv_w, conv_b, mh_g, q_norm_g, kv_norm_g, w_uq, w_uk, w_uv, w_pa, w_pb, w_o,
           ffn1_up, ffn1_down, ffn2_up, ffn2_down, ln_g, ln_b):
    x = jnp.concatenate([x_prompt.reshape(N_REAL, D_MODEL),
                         jnp.tile(meta.astype(F32), (BATCH, 1)),
                         x_sample.reshape(DEC_BATCH, D_MODEL),
                         jnp.zeros((TM - TAIL_ROWS, D_MODEL), F32)], axis=0)
    tabs = _rope_tabs()
    e_rope = _rope_select()
    pt_flat = page_table.reshape(-1)
    samp = slice(SAMP_OFF, SAMP_OFF + DEC_BATCH)
    metas = slice(META_OFF, SAMP_OFF)

    def seq_order(a, width):
        return jnp.concatenate([a[metas].reshape(BATCH, N_META, width), a[:N_REAL].reshape(BATCH, SEQ, width)],
                               axis=1)

    ffn1_up_b, ffn1_down_b = ffn1_up.astype(BF16), ffn1_down.astype(BF16)
    ffn2_up_b, ffn2_down_b = ffn2_up.astype(BF16), ffn2_down.astype(BF16)
    w_pa_b, w_pb_b, w_o_b = w_pa.astype(BF16), w_pb.astype(BF16), w_o.astype(BF16)
    ln_g3 = ln_g.reshape(DEPTH * 3, 1, D_MODEL)
    ln_b3 = ln_b.reshape(DEPTH * 3, 1, D_MODEL)
    state_m4 = state_m[:, :, None, :]
    cache_krope_t = jnp.swapaxes(cache_krope, 2, 3)
    conv_rows = (np.arange(BATCH)[:, None] * SEQ + (SEQ - (CONV_W - 1)) + np.arange(CONV_W - 1)[None, :]).reshape(-1)

    st = {k: [] for k in ("ckv_p", "kr_p", "c_p", "n_p", "m_p", "conv_p", "ckv_s", "kr_s", "c_s", "n_s", "m_s",
                          "conv_s")}
    for l in range(DEPTH):
        wp = _prep_layer(l, w_in, b_in, q_norm_g, kv_norm_g, w_uq, w_uk, w_uv)
        x = _ffn(x, l, ffn1_up_b, ffn1_down_b, 3 * l, ln_g3, ln_b3)
        q, k, vb, ckvn, sg, qk, vm, om, ga, gb = _inproj(x, wp, tabs)

        hn, hn_meta, c_p, n_p, m_p = mlstm_prompt_from_rows(qk, vm, om, sg[:, :2 * A_HEADS], conv_w[l], conv_b[l],
                                                            mh_g[l])
        hn_s, c_s, n_s, m_s = _mlstm_sample(
            l, qk[samp][:, None, :], state_conv, vm[samp].astype(F32)[:, None, :], om[samp][:, None, :],
            sg[samp][:, None, :], state_m4, state_C, state_n,
            conv_w[l], conv_b[l][None], mh_g[l][None])
        hn_tail = _tail_tile(hn_meta, hn_s.reshape(DEC_BATCH, A_VW))

        ob = _attn_prompt(q, k, vb)
        ob_meta = _attn_meta(q, k, vb)
        ob_s = _attn_sample(l, pt_flat, q[samp].astype(F32)[:, None, :], ckvn[samp][:, None, :],
                            sg[samp][:, None, :], wp["wk_ext"], e_rope, wp["w_uv"], cache_ckv, cache_krope_t)
        ob_tail = _tail_tile(ob_meta, ob_s.reshape(DEC_BATCH, B_HEADS * B_DV))

        x = _merge_ffn(x, hn, hn_tail, ob, ob_tail, ga, gb, l, w_pa_b, w_pb_b, w_o_b, ffn2_up_b, ffn2_down_b,
                       3 * l + 1, ln_g3, ln_b3)

        kr_all = sg[:, ROPE_LANE:ROPE_LANE + B_ROPE]
        st["ckv_p"].append(seq_order(ckvn, KV_RANK))
        st["kr_p"].append(seq_order(kr_all, B_ROPE))
        st["c_p"].append(c_p)
        st["n_p"].append(n_p)
        st["m_p"].append(m_p)
        st["conv_p"].append(jnp.take(qk, conv_rows, axis=0).reshape(BATCH, CONV_W - 1, 2 * A_QK))
        st["ckv_s"].append(ckvn[samp][:, None, :])
        st["kr_s"].append(kr_all[samp][:, None, :])
        st["c_s"].append(c_s)
        st["n_s"].append(n_s)
        st["m_s"].append(m_s[:, 0, :])
        st["conv_s"].append(jnp.concatenate([state_conv[l][:, 1:], qk[samp][:, None, :]], axis=1))

    y_prompt = x[:N_REAL].reshape(BATCH, SEQ, D_MODEL)
    y_sample = x[samp].reshape(DEC_BATCH, 1, D_MODEL)
    s = {k: jnp.stack(v) for k, v in st.items()}
    return (y_prompt, y_sample, s["ckv_p"], s["kr_p"], s["c_p"], s["n_p"], s["m_p"], s["conv_p"],
            s["ckv_s"], s["kr_s"], s["c_s"], s["n_s"], s["m_s"], s["conv_s"])
```

```python
import functools

import numpy as np
import jax
import jax.numpy as jnp
from jax import lax
from jax.experimental import pallas as pl
from jax.experimental.pallas import tpu as pltpu

F32 = jnp.float32
BF16 = jnp.bfloat16

D_MODEL = 1024
BATCH = 8
SEQ = 2048
DEPTH = 4
DEC_BATCH = 32
PAST_LEN = 16384
PAGE_SIZE = 128
N_PAGES = PAST_LEN // PAGE_SIZE
N_META = 16
A_HEADS = 4
A_DK = 128
A_DV = 128
A_QK = A_HEADS * A_DK
A_VW = A_HEADS * A_DV
CONV_W = 4
B_HEADS = 8
B_NOPE = 64
B_ROPE = 32
B_DV = 64
Q_RANK = 384
KV_RANK = 256
ROPE_BASE = 10000.0
ATTN_SCALE = (B_NOPE + B_ROPE) ** -0.5
LOG2E = 1.4426950408889634
Q_SCALE = ATTN_SCALE * LOG2E
D_FF = 2816
DN_ALPHA = (2 * DEPTH) ** 0.25
LN_EPS = 1e-5
RMS_EPS = 1e-6
IN_SIZES = (Q_RANK, KV_RANK, B_ROPE, 2 * A_QK, A_VW, A_VW, A_HEADS, A_HEADS, D_MODEL, D_MODEL)

LANE = 128
HALF_ROPE = B_ROPE // 2
ROPE_LANE = B_NOPE

TM = 512
N_REAL = BATCH * SEQ
META_OFF = N_REAL
SAMP_OFF = META_OFF + BATCH * N_META
TAIL_ROWS = BATCH * N_META + DEC_BATCH
T_PAD = N_REAL + TM
N_TILES = T_PAD // TM
REAL_TILES_PER_SEQ = SEQ // TM
S_ALL = SEQ + N_META

G_CQ, G_CKV, G_SG, G_QK, G_V, G_O, G_GA, G_GB = 0, 384, 640, 768, 1792, 2304, 2816, 3840
N_IN_P = 4864

FF_CHUNKS = 2
FF_C = D_FF // FF_CHUNKS

MLSTM_L = 256
MLSTM_HP = 2
MLSTM_SB = 8
CONV_PAD = 8

ATT_TQ = 512
ATT_TK = 512
PAGES_PER_STEP = 32
KEYS_PER_STEP = PAGES_PER_STEP * PAGE_SIZE

VMEM_LIMIT = 56 * 1024 * 1024


def _cparams(sem):
    return pltpu.CompilerParams(dimension_semantics=sem, vmem_limit_bytes=VMEM_LIMIT)


def _const_spec(shape):
    nd = len(shape)
    return pl.BlockSpec(shape, lambda *_: (0,) * nd, pipeline_mode=pl.Buffered(1))


def _layer_spec(layer, shape):
    nd = len(shape)
    return pl.BlockSpec((None,) + shape, lambda *_: (layer,) + (0,) * nd, pipeline_mode=pl.Buffered(1))


def _layer_norm(z, g, b):
    mu = jnp.mean(z, axis=-1, keepdims=True)
    zc = z - mu
    var = jnp.mean(zc * zc, axis=-1, keepdims=True)
    return zc * lax.rsqrt(var + LN_EPS) * g + b


def _rms_norm(z, g):
    return z * lax.rsqrt(jnp.mean(z * z, axis=-1, keepdims=True) + RMS_EPS) * g


def _dot(a, b):
    return jnp.dot(a, b, preferred_element_type=F32)


def _dot_nt(a, b):
    return lax.dot_general(a, b, (((1,), (1,)), ((), ())), preferred_element_type=F32)


def _dot_tn(a, b):
    return lax.dot_general(a, b, (((0,), (0,)), ((), ())), preferred_element_type=F32)


def _log_sigmoid(x):
    return jnp.minimum(x, 0.0) - jnp.log1p(jnp.exp(-jnp.abs(x)))


def _swiglu_postnorm(x, wup_ref, wdn_ref, g, b):
    xb = x.astype(BF16)
    y = jnp.zeros_like(x)
    for c in range(FF_CHUNKS):
        lo = c * FF_C
        a = _dot(xb, wup_ref[:, lo:lo + FF_C])
        u = _dot(xb, wup_ref[:, D_FF + lo:D_FF + lo + FF_C])
        h = (a * jax.nn.sigmoid(a) * u).astype(BF16)
        y = y + _dot(h, wdn_ref[lo:lo + FF_C, :])
    return _layer_norm(DN_ALPHA * x + 0.5 * y, g, b)


def _ffn_kernel(x_ref, wup_ref, wdn_ref, g_ref, b_ref, o_ref):
    o_ref[...] = _swiglu_postnorm(x_ref[...], wup_ref, wdn_ref, g_ref[...], b_ref[...])


def _ffn_split_kernel(xm_ref, xt_ref, wup_ref, wdn_ref, g_ref, b_ref, o_ref):
    x = jnp.where(pl.program_id(0) == N_TILES - 1, xt_ref[...], xm_ref[...])
    o_ref[...] = _swiglu_postnorm(x, wup_ref, wdn_ref, g_ref[...], b_ref[...])


def _ffn(x, layer, w_up, w_down, ln_idx, g, b, x_tail=None):
    row = pl.BlockSpec((TM, D_MODEL), lambda i: (i, 0))
    weights = [_layer_spec(layer, (D_MODEL, 2 * D_FF)), _layer_spec(layer, (D_FF, D_MODEL)),
               _layer_spec(ln_idx, (1, D_MODEL)), _layer_spec(ln_idx, (1, D_MODEL))]
    if x_tail is None:
        body, x_specs, xs = _ffn_kernel, [row], (x,)
    else:
        body, xs = _ffn_split_kernel, (x, x_tail)
        x_specs = [pl.BlockSpec((TM, D_MODEL), lambda i: (jnp.minimum(i, N_REAL // TM - 1), 0)),
                   pl.BlockSpec((TM, D_MODEL), lambda i: (0, 0))]
    return pl.pallas_call(
        body,
        grid=(N_TILES,),
        in_specs=x_specs + weights,
        out_specs=row,
        out_shape=jax.ShapeDtypeStruct((T_PAD, D_MODEL), F32),
        compiler_params=_cparams(("parallel",)),
        name="ffn_postnorm",
    )(*xs, w_up, w_down, g, b)


def _rope(z, tc, ts1, ts2):
    return z * tc + pltpu.roll(z, LANE - HALF_ROPE, 1) * ts1 + pltpu.roll(z, HALF_ROPE, 1) * ts2


def _inproj_kernel(x_ref, w_ref, b_ref, qg_ref, kvg_ref, wuq_ref, wkf_ref, wuv_ref, tc_ref, ts1_ref, ts2_ref,
                   q_ref, k_ref, vb_ref, ckv_ref, sg_ref, qk_ref, vm_ref, om_ref, ga_ref, gb_ref):
    xb = x_ref[...].astype(BF16)

    def proj(off, n):
        return _dot(xb, w_ref[:, off:off + n]) + b_ref[:, off:off + n]

    tc, ts1, ts2 = tc_ref[...], ts1_ref[...], ts2_ref[...]
    cqn = _rms_norm(proj(G_CQ, Q_RANK), qg_ref[...]).astype(BF16)
    q = _dot(cqn, wuq_ref[...])
    for h in range(B_HEADS):
        qh = _rope(q[:, h * LANE:(h + 1) * LANE], tc, ts1, ts2)
        q_ref[:, h * LANE:(h + 1) * LANE] = (qh * Q_SCALE).astype(BF16)
    ckvn = _rms_norm(proj(G_CKV, KV_RANK), kvg_ref[...])
    ckv_ref[...] = ckvn
    sg = _rope(proj(G_SG, LANE), tc, ts1, ts2)
    lane = lax.broadcasted_iota(jnp.int32, sg.shape, 1)
    sg = jnp.where((lane >= A_HEADS) & (lane < 2 * A_HEADS), _log_sigmoid(sg), sg)
    sg_ref[...] = sg
    kin = jnp.concatenate([ckvn.astype(BF16), sg.astype(BF16)], axis=1)
    k_ref[...] = _dot(kin, wkf_ref[...]).astype(BF16)
    vb_ref[...] = _dot(kin[:, :KV_RANK], wuv_ref[...]).astype(BF16)
    qk_ref[...] = proj(G_QK, 2 * A_QK)
    vm_ref[...] = proj(G_V, A_VW).astype(BF16)
    om_ref[...] = proj(G_O, A_VW)
    ga_ref[...] = jax.nn.sigmoid(proj(G_GA, D_MODEL)).astype(BF16)
    gb_ref[...] = jax.nn.sigmoid(proj(G_GB, D_MODEL)).astype(BF16)


def _inproj(x, wp, tabs):
    def row(n):
        return pl.BlockSpec((TM, n), lambda i: (i, 0))

    tab = pl.BlockSpec((TM, LANE), lambda i: (jnp.where(i < N_REAL // TM, i % REAL_TILES_PER_SEQ,
                                                       REAL_TILES_PER_SEQ), 0))
    widths = (B_HEADS * LANE, B_HEADS * LANE, B_HEADS * B_DV, KV_RANK, LANE, 2 * A_QK, A_VW, A_VW, D_MODEL, D_MODEL)
    dtypes = (BF16, BF16, BF16, F32, F32, F32, BF16, F32, BF16, BF16)
    return pl.pallas_call(
        _inproj_kernel,
        grid=(N_TILES,),
        in_specs=[row(D_MODEL), _const_spec((D_MODEL, N_IN_P)), _const_spec((1, N_IN_P)),
                  _const_spec((1, Q_RANK)), _const_spec((1, KV_RANK)),
                  _const_spec((Q_RANK, B_HEADS * LANE)), _const_spec((KV_RANK + LANE, B_HEADS * LANE)),
                  _const_spec((KV_RANK, B_HEADS * B_DV)), tab, tab, tab],
        out_specs=[row(n) for n in widths],
        out_shape=[jax.ShapeDtypeStruct((T_PAD, n), d) for n, d in zip(widths, dtypes)],
        compiler_params=_cparams(("parallel",)),
        name="in_proj",
    )(x, wp["w_in"], wp["b_in"], wp["qg"], wp["kvg"], wp["w_uq"], wp["w_kf"], wp["w_uv"], *tabs)


def _mlstm_prompt_kernel(qm_ref, km_ref, q0_ref, k0_ref, vm_ref, v0_ref, om_ref, o0_ref, gr_ref, gc_ref,
                         cwq_ref, cwk_ref, cbq_ref, cbk_ref, mhg_ref,
                         hm_ref, h0_ref, cx_ref, m_ref, uq_ref, uk_ref, vx_ref):
    for u_ref, a0_ref, am_ref in ((uq_ref, q0_ref, qm_ref), (uk_ref, k0_ref, km_ref)):
        u_ref[0:CONV_PAD, :] = jnp.zeros((CONV_PAD, MLSTM_HP * LANE), F32)
        u_ref[CONV_PAD:CONV_PAD + N_META, :] = a0_ref[...]
        u_ref[CONV_PAD + N_META:, :] = am_ref[...]
    for hh in range(MLSTM_HP):
        lanes = slice(hh * LANE, (hh + 1) * LANE)
        vx_ref[hh, 0:N_META, 0:LANE] = v0_ref[:, lanes]
        vx_ref[hh, N_META:, 0:LANE] = vm_ref[:, lanes]
        vx_ref[hh, :, LANE:] = jnp.ones((S_ALL, LANE), BF16)

    def conv_silu(u_ref, w_ref, b_ref, t0, length):
        acc = b_ref[...]
        for j in range(CONV_W):
            lo = CONV_PAD - (CONV_W - 1) + j + t0
            acc = acc + w_ref[j:j + 1, :] * u_ref[lo:lo + length, :]
        return acc * jax.nn.sigmoid(acc)

    def head_chunk(hh, q, k, t0, length, o_ref, h_ref, r0, cx, m):
        lanes = slice(hh * LANE, (hh + 1) * LANE)
        qb, kb = q.astype(BF16), k.astype(BF16)
        ig_r = gr_ref[hh, 0:1, t0:t0 + length]
        lf_r = gr_ref[hh, 1:2, t0:t0 + length]
        ig_c = gc_ref[hh, t0:t0 + length, 0:1]
        lf_c = gc_ref[hh, t0:t0 + length, 1:2]
        row = lax.broadcasted_iota(jnp.int32, (length, length), 0)
        col = lax.broadcasted_iota(jnp.int32, (length, length), 1)
        tri = row >= col
        b_c = jnp.sum(jnp.where(tri, lf_r, 0.0), axis=1, keepdims=True)
        b_r = jnp.sum(jnp.where(row <= col, lf_c, 0.0), axis=0, keepdims=True)
        dmat = jnp.where(tri, b_c - b_r + ig_r, -jnp.inf)
        inter = b_c + m
        m_row = jnp.maximum(inter, jnp.max(dmat, axis=1, keepdims=True))
        s = _dot_nt(qb, kb) * jnp.exp(dmat - m_row)
        w_prev = jnp.exp(inter - m_row)
        vx = vx_ref[hh, t0:t0 + length, :]
        tot = w_prev * _dot(qb, cx.astype(BF16)) + _dot(s.astype(BF16), vx)
        h = tot[:, :A_DV] / jnp.maximum(jnp.abs(tot[:, A_DV:]), jnp.exp(-m_row))
        b_last = b_c[length - 1:length, :]
        g = b_last - b_c + ig_c
        m_new = jnp.maximum(b_last + m, jnp.max(g, axis=0, keepdims=True))
        decay = jnp.exp(b_last + m - m_new)
        wk = (jnp.exp(g - m_new) * k).astype(BF16)
        cx_new = decay * cx + _dot_tn(wk, vx)
        mu = jnp.mean(h, axis=-1, keepdims=True)
        hc = h - mu
        var = jnp.mean(hc * hc, axis=-1, keepdims=True)
        hn = hc * lax.rsqrt(var + LN_EPS) * mhg_ref[:, lanes] * jax.nn.sigmoid(o_ref[r0:r0 + length, lanes])
        h_ref[r0:r0 + length, lanes] = hn.astype(BF16)
        return cx_new, m_new

    def chunk(t0, length, o_ref, h_ref, r0, states):
        q2 = conv_silu(uq_ref, cwq_ref, cbq_ref, t0, length)
        k2 = conv_silu(uk_ref, cwk_ref, cbk_ref, t0, length) * (A_DK ** -0.5)
        return [head_chunk(hh, q2[:, hh * LANE:(hh + 1) * LANE], k2[:, hh * LANE:(hh + 1) * LANE],
                           t0, length, o_ref, h_ref, r0, *states[hh]) for hh in range(MLSTM_HP)]

    states = [(jnp.zeros((A_DK, 2 * LANE), F32), jnp.zeros((1, 1), F32)) for _ in range(MLSTM_HP)]
    states = chunk(0, N_META, o0_ref, h0_ref, 0, states)
    for c in range(SEQ // MLSTM_L):
        states = chunk(N_META + c * MLSTM_L, MLSTM_L, om_ref, hm_ref, c * MLSTM_L, states)
    for hh, (cx, m) in enumerate(states):
        cx_ref[hh] = cx
        m_ref[hh] = jnp.broadcast_to(m, (1, LANE))


def _mlstm_prompt(qk, vm, om, gr, gc, conv_w, conv_b, mh_g):
    meta_blk = META_OFF // N_META
    width = MLSTM_HP * LANE
    k_off = A_HEADS // MLSTM_HP

    def main(off):
        return pl.BlockSpec((SEQ, width), lambda b, j: (b, j + off))

    def meta(off):
        return pl.BlockSpec((N_META, width), lambda b, j: (meta_blk + b, j + off))

    def wcol(rows, off):
        return pl.BlockSpec((rows, width), lambda b, j: (0, j + off))

    def per_head(*shape):
        nd = len(shape)
        return pl.BlockSpec((None, MLSTM_HP) + shape, lambda b, j: (b, j) + (0,) * nd)

    return pl.pallas_call(
        _mlstm_prompt_kernel,
        grid=(BATCH, A_HEADS // MLSTM_HP),
        in_specs=[main(0), main(k_off), meta(0), meta(k_off), main(0), meta(0), main(0), meta(0),
                  per_head(2, S_ALL), per_head(S_ALL, 2),
                  wcol(CONV_W, 0), wcol(CONV_W, k_off), wcol(1, 0), wcol(1, k_off), wcol(1, 0)],
        out_specs=[main(0),
                   pl.BlockSpec((N_META, width), lambda b, j: (b, j)),
                   per_head(A_DK, 2 * LANE), per_head(1, LANE)],
        out_shape=[jax.ShapeDtypeStruct((N_REAL, A_VW), BF16),
                   jax.ShapeDtypeStruct((BATCH * N_META, A_VW), BF16),
                   jax.ShapeDtypeStruct((BATCH, A_HEADS, A_DK, 2 * LANE), F32),
                   jax.ShapeDtypeStruct((BATCH, A_HEADS, 1, LANE), F32)],
        scratch_shapes=[pltpu.VMEM((CONV_PAD + S_ALL, width), F32), pltpu.VMEM((CONV_PAD + S_ALL, width), F32),
                        pltpu.VMEM((MLSTM_HP, S_ALL, 2 * LANE), BF16)],
        compiler_params=_cparams(("parallel", "parallel")),
        name="mlstm_prompt",
    )(qk, qk, qk, qk, vm, vm, om, om, gr, gc, conv_w, conv_w, conv_b, conv_b, mh_g)


def mlstm_prompt_from_rows(qk, vm, om, gates, conv_w, conv_b, mh_g):
    g = jnp.concatenate([gates[META_OFF:SAMP_OFF].reshape(BATCH, N_META, 2, A_HEADS),
                         gates[:N_REAL].reshape(BATCH, SEQ, 2, A_HEADS)], axis=1)
    gr = jnp.transpose(g, (0, 3, 2, 1))
    gc = jnp.transpose(g, (0, 3, 1, 2))
    hn, hn_meta, cx, m = _mlstm_prompt(qk, vm, om, gr, gc, conv_w, conv_b[None], mh_g[None])
    return hn, hn_meta, cx[..., :A_DV], cx[..., A_DV], m[:, :, 0, 0]


def _mlstm_sample_kernel(qk_ref, cprev_ref, v_ref, o_ref, sg_ref, m_ref, c_ref, n_ref, cw_ref, cb_ref, mhg_ref,
                         hn_ref, cnew_ref, nnew_ref, mnew_ref):
    for i in range(MLSTM_SB):
        _mlstm_sample_row(i, qk_ref, cprev_ref, v_ref, o_ref, sg_ref, m_ref, c_ref, n_ref, cw_ref, cb_ref, mhg_ref,
                          hn_ref, cnew_ref, nnew_ref, mnew_ref)


def _mlstm_sample_row(i, qk_ref, cprev_ref, v_ref, o_ref, sg_ref, m_ref, c_ref, n_ref, cw_ref, cb_ref, mhg_ref,
                      hn_ref, cnew_ref, nnew_ref, mnew_ref):
    acc = (cb_ref[...] + jnp.sum(cw_ref[0:CONV_W - 1, :] * cprev_ref[i], axis=0, keepdims=True)
           + cw_ref[CONV_W - 1:CONV_W, :] * qk_ref[i])
    a = acc * jax.nn.sigmoid(acc)
    sg = sg_ref[i]
    m_in = m_ref[i]
    row8 = lax.broadcasted_iota(jnp.int32, (8, LANE), 0)
    for h in range(A_HEADS):
        q = a[:, h * A_DK:(h + 1) * A_DK]
        k = a[:, A_QK + h * A_DK:A_QK + (h + 1) * A_DK] * (A_DK ** -0.5)
        v = v_ref[i, :, h * A_DV:(h + 1) * A_DV]
        ig = sg[:, h:h + 1]
        lf = sg[:, A_HEADS + h:A_HEADS + h + 1]
        m = m_in[:, h:h + 1]
        c = c_ref[i, h]
        n = n_ref[i, h:h + 1, :]
        inter = lf + m
        m_row = jnp.maximum(inter, ig)
        s = jnp.sum(q * k, axis=-1, keepdims=True) * jnp.exp(ig - m_row)
        w_prev = jnp.exp(inter - m_row)
        q8 = jnp.broadcast_to(q, (8, A_DK)).astype(BF16)
        qc = _dot(q8, c.astype(BF16))[0:1, :]
        num = w_prev * qc + s * v
        den = w_prev * jnp.sum(q * n, axis=-1, keepdims=True) + s
        hh = num / jnp.maximum(jnp.abs(den), jnp.exp(-m_row))
        wk = jnp.exp(ig - m_row) * k
        wk8 = jnp.where(row8 == 0, jnp.broadcast_to(wk, (8, A_DK)), 0.0).astype(BF16)
        v8 = jnp.broadcast_to(v, (8, A_DV)).astype(BF16)
        cnew_ref[i, h] = w_prev * c + _dot_tn(wk8, v8)
        nnew_ref[i, h:h + 1, :] = w_prev * n + wk
        mnew_ref[i, :, h:h + 1] = m_row
        mu = jnp.mean(hh, axis=-1, keepdims=True)
        hc = hh - mu
        var = jnp.mean(hc * hc, axis=-1, keepdims=True)
        hn = hc * lax.rsqrt(var + LN_EPS) * mhg_ref[:, h * A_DV:(h + 1) * A_DV]
        hn_ref[i, :, h * A_DV:(h + 1) * A_DV] = hn * jax.nn.sigmoid(o_ref[i, :, h * A_DV:(h + 1) * A_DV])


def _mlstm_sample(layer, qk_s, conv_prev, v_s, o_s, sg_s, m_s, state_c, state_n, conv_w, conv_b, mh_g):
    def per_b(*shape):
        nd = len(shape)
        return pl.BlockSpec((MLSTM_SB,) + shape, lambda b: (b,) + (0,) * nd)

    def per_lb(*shape):
        nd = len(shape)
        return pl.BlockSpec((None, MLSTM_SB) + shape, lambda b: (layer, b) + (0,) * nd)

    return pl.pallas_call(
        _mlstm_sample_kernel,
        grid=(DEC_BATCH // MLSTM_SB,),
        in_specs=[per_b(1, 2 * A_QK), per_lb(CONV_W - 1, 2 * A_QK), per_b(1, A_VW), per_b(1, A_VW), per_b(1, LANE),
                  per_lb(1, A_HEADS), per_lb(A_HEADS, A_DK, A_DV), per_lb(A_HEADS, A_DK),
                  _const_spec((CONV_W, 2 * A_QK)), _const_spec((1, 2 * A_QK)), _const_spec((1, A_VW))],
        out_specs=[per_b(1, A_VW), per_b(A_HEADS, A_DK, A_DV), per_b(A_HEADS, A_DK), per_b(1, A_HEADS)],
        out_shape=[jax.ShapeDtypeStruct((DEC_BATCH, 1, A_VW), F32),
                   jax.ShapeDtypeStruct((DEC_BATCH, A_HEADS, A_DK, A_DV), F32),
                   jax.ShapeDtypeStruct((DEC_BATCH, A_HEADS, A_DK), F32),
                   jax.ShapeDtypeStruct((DEC_BATCH, 1, A_HEADS), F32)],
        compiler_params=_cparams(("parallel",)),
        name="mlstm_sample",
    )(qk_s, conv_prev, v_s, o_s, sg_s, m_s, state_c, state_n, conv_w, conv_b, mh_g)


def _attn_prompt_kernel(q_ref, k_ref, v_ref, k0_ref, v0_ref, o_ref):
    i = pl.program_id(2)
    sub = ATT_TQ // ATT_TK
    head_lanes = [slice(hh * LANE, (hh + 1) * LANE) for hh in range(2)]

    def head_tile(lanes, state, r0, diag):
        m, l, acc = state
        s = _dot_nt(q_ref[:, lanes], k_ref[pl.ds(r0, ATT_TK), lanes])
        if diag is not None:
            row = lax.broadcasted_iota(jnp.int32, (ATT_TQ, ATT_TK), 0)
            col = lax.broadcasted_iota(jnp.int32, (ATT_TQ, ATT_TK), 1)
            s = jnp.where(col + diag * ATT_TK <= row, s, -jnp.inf)
        m_new = jnp.maximum(m, jnp.max(s, axis=-1, keepdims=True))
        alpha = jnp.exp2(m - m_new)
        p = jnp.exp2(s - m_new)
        l = alpha * l + jnp.sum(p, axis=-1, keepdims=True)
        acc = alpha * acc + _dot(p.astype(BF16), v_ref[pl.ds(r0, ATT_TK), :])
        return m_new, l, acc

    def kv_tile(kt, carry, diag):
        r0 = pl.multiple_of(kt * ATT_TK, ATT_TK)
        return tuple(head_tile(lanes, st, r0, diag) for lanes, st in zip(head_lanes, carry))

    init = []
    for lanes in head_lanes:
        s0 = _dot_nt(q_ref[:, lanes], k0_ref[:, lanes])
        m = jnp.max(s0, axis=-1, keepdims=True)
        p0 = jnp.exp2(s0 - m)
        init.append((m, jnp.sum(p0, axis=-1, keepdims=True), _dot(p0.astype(BF16), v0_ref[...])))
    carry = lax.fori_loop(0, i * sub, functools.partial(kv_tile, diag=None), tuple(init))
    for d in range(sub):
        carry = kv_tile(i * sub + d, carry, d)
    outs = [acc / l for _, l, acc in carry]
    lane = lax.broadcasted_iota(jnp.int32, (ATT_TQ, LANE), 1)
    o_ref[...] = jnp.where(lane < B_DV, outs[0], outs[1]).astype(BF16)


def _attn_prompt(q, k, vb):
    assert ATT_TQ % ATT_TK == 0
    nq = SEQ // ATT_TQ
    meta_blk = META_OFF // N_META
    return pl.pallas_call(
        _attn_prompt_kernel,
        grid=(BATCH, B_HEADS // 2, nq),
        in_specs=[pl.BlockSpec((ATT_TQ, 2 * LANE), lambda b, j, i: (b * nq + i, j)),
                  pl.BlockSpec((SEQ, 2 * LANE), lambda b, j, i: (b, j)),
                  pl.BlockSpec((SEQ, LANE), lambda b, j, i: (b, j)),
                  pl.BlockSpec((N_META, 2 * LANE), lambda b, j, i: (meta_blk + b, j)),
                  pl.BlockSpec((N_META, LANE), lambda b, j, i: (meta_blk + b, j))],
        out_specs=pl.BlockSpec((ATT_TQ, LANE), lambda b, j, i: (b * nq + i, j)),
        out_shape=jax.ShapeDtypeStruct((N_REAL, B_HEADS * B_DV), BF16),
        compiler_params=_cparams(("parallel", "parallel", "parallel")),
        name="attn_prompt",
    )(q, k, vb, k, vb)


def _attn_meta_kernel(q_ref, k_ref, v_ref, o_ref):
    row = lax.broadcasted_iota(jnp.int32, (N_META, N_META), 0)
    col = lax.broadcasted_iota(jnp.int32, (N_META, N_META), 1)
    lane = lax.broadcasted_iota(jnp.int32, (N_META, B_HEADS * B_DV), 1)
    out = jnp.zeros((N_META, B_HEADS * B_DV), F32)
    for h in range(B_HEADS):
        lanes = slice(h * LANE, (h + 1) * LANE)
        s = _dot_nt(q_ref[:, lanes], k_ref[:, lanes])
        s = jnp.where(col <= row, s, -jnp.inf)
        p = jnp.exp2(s - jnp.max(s, axis=-1, keepdims=True))
        l = jnp.sum(p, axis=-1, keepdims=True)
        o = _dot(p.astype(BF16), v_ref[...]) / l
        out = jnp.where(lane // B_DV == h, o, out)
    o_ref[...] = out.astype(BF16)


def _attn_meta(q, k, vb):
    meta_blk = META_OFF // N_META

    def spec(n):
        return pl.BlockSpec((N_META, n), lambda b: (meta_blk + b, 0))

    return pl.pallas_call(
        _attn_meta_kernel,
        grid=(BATCH,),
        in_specs=[spec(B_HEADS * LANE), spec(B_HEADS * LANE), spec(B_HEADS * B_DV)],
        out_specs=pl.BlockSpec((N_META, B_HEADS * B_DV), lambda b: (b, 0)),
        out_shape=jax.ShapeDtypeStruct((BATCH * N_META, B_HEADS * B_DV), BF16),
        compiler_params=_cparams(("parallel",)),
        name="attn_meta",
    )(q, k, vb)


def _attn_sample_kernel(pt_ref, q_ref, ckvn_ref, sg_ref, wk_ref, er_ref, wuv_ref, *rest):
    ck_refs = rest[:PAGES_PER_STEP]
    kr_refs = rest[PAGES_PER_STEP:2 * PAGES_PER_STEP]
    o_ref, qlat_ref, qr_ref, m_ref, l_ref, acc_ref, kb_ref, krb_ref = rest[2 * PAGES_PER_STEP:]
    p_idx = pl.program_id(1)

    @pl.when(p_idx == 0)
    def _init():
        q = q_ref[...]
        row = lax.broadcasted_iota(jnp.int32, (B_HEADS, B_HEADS * LANE), 0)
        lane = lax.broadcasted_iota(jnp.int32, (B_HEADS, B_HEADS * LANE), 1)
        qbd = jnp.where(lane // LANE == row, jnp.broadcast_to(q, (B_HEADS, B_HEADS * LANE)), 0.0).astype(BF16)
        qlat = _dot(qbd, wk_ref[...])
        qr = _dot(qbd, er_ref[...])
        qlat_ref[...] = qlat.astype(BF16)
        qr_ref[...] = qr.astype(BF16)
        ckvn = ckvn_ref[...]
        s_new = (jnp.sum(qlat * ckvn, axis=-1, keepdims=True)
                 + jnp.sum(qr * sg_ref[...], axis=-1, keepdims=True))
        m_ref[...] = s_new
        l_ref[...] = jnp.ones_like(s_new)
        acc_ref[...] = jnp.broadcast_to(ckvn, (B_HEADS, KV_RANK))

    for g in range(PAGES_PER_STEP):
        kb_ref[g * PAGE_SIZE:(g + 1) * PAGE_SIZE, :] = ck_refs[g][...].astype(BF16)
        krb_ref[:, g * PAGE_SIZE:(g + 1) * PAGE_SIZE] = kr_refs[g][...].astype(BF16)
    qr = qr_ref[:, ROPE_LANE:ROPE_LANE + B_ROPE]
    s = _dot_nt(qlat_ref[...], kb_ref[...]) + _dot(qr, krb_ref[...])
    m_old = m_ref[...]
    m_new = jnp.maximum(m_old, jnp.max(s, axis=-1, keepdims=True))
    alpha = jnp.exp2(m_old - m_new)
    p = jnp.exp2(s - m_new)
    l_ref[...] = alpha * l_ref[...] + jnp.sum(p, axis=-1, keepdims=True)
    acc_ref[...] = alpha * acc_ref[...] + _dot(p.astype(BF16), kb_ref[...])
    m_ref[...] = m_new

    @pl.when(p_idx == pl.num_programs(1) - 1)
    def _fin():
        o_lat = (acc_ref[...] / l_ref[...]).astype(BF16)
        o_all = _dot(o_lat, wuv_ref[...])
        row = lax.broadcasted_iota(jnp.int32, o_all.shape, 0)
        lane = lax.broadcasted_iota(jnp.int32, o_all.shape, 1)
        o_ref[...] = jnp.sum(jnp.where(lane // B_DV == row, o_all, 0.0), axis=0, keepdims=True)


def _attn_sample(layer, page_table, q_s, ckvn_s, sg_s, wk_ext, e_rope, w_uv, cache_ckv, cache_krope_t):
    steps = N_PAGES // PAGES_PER_STEP

    def per_b(n):
        return pl.BlockSpec((None, 1, n), lambda b, p, pt: (b, 0, 0))

    def const(shape):
        return pl.BlockSpec(shape, lambda b, p, pt: (0, 0), pipeline_mode=pl.Buffered(1))

    def page(rows, width, g):
        return pl.BlockSpec((None, None, rows, width),
                            lambda b, p, pt: (layer, pt[b * N_PAGES + p * PAGES_PER_STEP + g], 0, 0))

    grid_spec = pltpu.PrefetchScalarGridSpec(
        num_scalar_prefetch=1,
        grid=(DEC_BATCH, steps),
        in_specs=[per_b(B_HEADS * LANE), per_b(KV_RANK), per_b(LANE),
                  const((B_HEADS * LANE, KV_RANK)), const((B_HEADS * LANE, LANE)), const((KV_RANK, B_HEADS * B_DV))]
                 + [page(PAGE_SIZE, KV_RANK, g) for g in range(PAGES_PER_STEP)]
                 + [page(B_ROPE, PAGE_SIZE, g) for g in range(PAGES_PER_STEP)],
        out_specs=pl.BlockSpec((None, 1, B_HEADS * B_DV), lambda b, p, pt: (b, 0, 0)),
        scratch_shapes=[pltpu.VMEM((B_HEADS, KV_RANK), BF16), pltpu.VMEM((B_HEADS, LANE), BF16),
                        pltpu.VMEM((B_HEADS, 1), F32), pltpu.VMEM((B_HEADS, 1), F32),
                        pltpu.VMEM((B_HEADS, KV_RANK), F32),
                        pltpu.VMEM((KEYS_PER_STEP, KV_RANK), BF16), pltpu.VMEM((B_ROPE, KEYS_PER_STEP), BF16)],
    )
    return pl.pallas_call(
        _attn_sample_kernel,
        grid_spec=grid_spec,
        out_shape=jax.ShapeDtypeStruct((DEC_BATCH, 1, B_HEADS * B_DV), F32),
        compiler_params=_cparams(("parallel", "arbitrary")),
        name="attn_sample",
    )(page_table, q_s, ckvn_s, sg_s, wk_ext, e_rope, w_uv,
      *([cache_ckv] * PAGES_PER_STEP), *([cache_krope_t] * PAGES_PER_STEP))


def _merge_ffn_kernel(x_ref, hnm_ref, hnt_ref, obm_ref, obt_ref, ga_ref, gb_ref, wpa_ref, wpb_ref, wo_ref,
                      g1_ref, b1_ref, wup_ref, wdn_ref, g2_ref, b2_ref, o_ref):
    is_tail = pl.program_id(0) == N_TILES - 1
    hn = jnp.where(is_tail, hnt_ref[...], hnm_ref[...])
    ob = jnp.where(is_tail, obt_ref[...], obm_ref[...])
    y_a = _dot(hn, wpa_ref[...])
    y_b = _dot(ob, wpb_ref[...])
    mix = ga_ref[...].astype(F32) * y_a + gb_ref[...].astype(F32) * y_b
    y = _dot(mix.astype(BF16), wo_ref[...])
    x = _layer_norm(DN_ALPHA * x_ref[...] + y, g1_ref[...], b1_ref[...])
    o_ref[...] = _swiglu_postnorm(x, wup_ref, wdn_ref, g2_ref[...], b2_ref[...])


def _merge_ffn(x, hn, hn_tail, ob, ob_tail, ga, gb, layer, w_pa, w_pb, w_o, w_up, w_down, ln_idx, g, b):
    def row(n):
        return pl.BlockSpec((TM, n), lambda i: (i, 0))

    def main(n):
        return pl.BlockSpec((TM, n), lambda i: (jnp.minimum(i, N_REAL // TM - 1), 0))

    def tail(n):
        return pl.BlockSpec((TM, n), lambda i: (0, 0))

    return pl.pallas_call(
        _merge_ffn_kernel,
        grid=(N_TILES,),
        in_specs=[row(D_MODEL), main(A_VW), tail(A_VW), main(B_HEADS * B_DV), tail(B_HEADS * B_DV),
                  row(D_MODEL), row(D_MODEL),
                  _layer_spec(layer, (A_VW, D_MODEL)), _layer_spec(layer, (B_HEADS * B_DV, D_MODEL)),
                  _layer_spec(layer, (D_MODEL, D_MODEL)), _layer_spec(ln_idx, (1, D_MODEL)),
                  _layer_spec(ln_idx, (1, D_MODEL)),
                  _layer_spec(layer, (D_MODEL, 2 * D_FF)), _layer_spec(layer, (D_FF, D_MODEL)),
                  _layer_spec(ln_idx + 1, (1, D_MODEL)), _layer_spec(ln_idx + 1, (1, D_MODEL))],
        out_specs=row(D_MODEL),
        out_shape=jax.ShapeDtypeStruct((T_PAD, D_MODEL), F32),
        compiler_params=_cparams(("parallel",)),
        name="merge_ffn",
    )(x, hn, hn_tail, ob, ob_tail, ga, gb, w_pa, w_pb, w_o, g, b, w_up, w_down, g, b)


def _prep_layer(l, w_in, b_in, q_norm_g, kv_norm_g, w_uq, w_uk, w_uv):
    offs = np.cumsum((0,) + IN_SIZES)

    def grp(a, i):
        return a[..., offs[i]:offs[i + 1]]

    def arrange(a):
        lead = a.shape[:-1]
        z = lambda n: jnp.zeros(lead + (n,), F32)
        sg = jnp.concatenate([grp(a, 6), grp(a, 7), z(ROPE_LANE - 2 * A_HEADS), grp(a, 2),
                              z(LANE - ROPE_LANE - B_ROPE)], axis=-1)
        return jnp.concatenate([grp(a, 0), grp(a, 1), sg, grp(a, 3), grp(a, 4), grp(a, 5), grp(a, 8), grp(a, 9)],
                               axis=-1)

    pad_q = LANE - B_NOPE - B_ROPE
    wuq = jnp.pad(w_uq[l], ((0, 0), (0, 0), (0, pad_q))).reshape(Q_RANK, B_HEADS * LANE)
    wuk = jnp.pad(w_uk[l], ((0, 0), (0, 0), (0, LANE - B_NOPE)))
    place = np.zeros((LANE, B_HEADS, LANE), np.float32)
    for e in range(B_ROPE):
        place[ROPE_LANE + e, :, ROPE_LANE + e] = 1.0
    w_kf = jnp.concatenate([wuk.reshape(KV_RANK, B_HEADS * LANE),
                            jnp.asarray(place).reshape(LANE, B_HEADS * LANE)], axis=0)
    wk_ext = jnp.transpose(wuk, (1, 2, 0)).reshape(B_HEADS * LANE, KV_RANK)
    return {
        "w_in": arrange(w_in[l]).astype(BF16),
        "b_in": arrange(b_in[l])[None, :],
        "qg": q_norm_g[l][None, :],
        "kvg": kv_norm_g[l][None, :],
        "w_uq": wuq.astype(BF16),
        "w_kf": w_kf.astype(BF16),
        "w_uv": w_uv[l].reshape(KV_RANK, B_HEADS * B_DV).astype(BF16),
        "wk_ext": wk_ext.astype(BF16),
    }


def _rope_select():
    e = np.zeros((B_HEADS, LANE, LANE), np.float32)
    for r in range(B_ROPE):
        e[:, ROPE_LANE + r, ROPE_LANE + r] = 1.0
    return jnp.asarray(e.reshape(B_HEADS * LANE, LANE)).astype(BF16)


def _rope_tabs():
    pos_real = N_META + np.arange(SEQ)
    pos_tail = np.zeros((TM,), np.int64)
    pos_tail[:BATCH * N_META] = np.arange(BATCH * N_META) % N_META
    pos_tail[BATCH * N_META:TAIL_ROWS] = PAST_LEN
    pos = jnp.asarray(np.concatenate([pos_real, pos_tail]), dtype=jnp.int32)
    inv = ROPE_BASE ** (-jnp.arange(0, B_ROPE, 2, dtype=F32) / B_ROPE)
    ang = pos.astype(F32)[:, None] * inv[None, :]
    cos, sin = jnp.cos(ang), jnp.sin(ang)
    n = pos.shape[0]
    z = lambda w: jnp.zeros((n, w), F32)
    tail = LANE - ROPE_LANE - B_ROPE
    tc = jnp.concatenate([jnp.ones((n, ROPE_LANE), F32), cos, cos, z(tail)], axis=1)
    ts1 = jnp.concatenate([z(ROPE_LANE), -sin, z(HALF_ROPE), z(tail)], axis=1)
    ts2 = jnp.concatenate([z(ROPE_LANE), z(HALF_ROPE), sin, z(tail)], axis=1)
    return tc, ts1, ts2


def _tail_tile(meta_rows, samp_rows):
    width = meta_rows.shape[1]
    return jnp.concatenate([meta_rows, samp_rows.astype(meta_rows.dtype),
                            jnp.zeros((TM - TAIL_ROWS, width), meta_rows.dtype)], axis=0)


def kernel(x_prompt, x_sample, cache_ckv, cache_krope, page_table, state_C, state_n, state_m, state_conv, meta,
           w_in, b_in, conv_w, conv_b, mh_g, q_norm_g, kv_norm_g, w_uq, w_uk, w_uv, w_pa, w_pb, w_o,
           ffn1_up, ffn1_down, ffn2_up, ffn2_down, ln_g, ln_b):
    x = x_prompt.reshape(N_REAL, D_MODEL)
    x_tail = _tail_tile(jnp.tile(meta.astype(F32), (BATCH, 1)), x_sample.reshape(DEC_BATCH, D_MODEL))
    tabs = _rope_tabs()
    e_rope = _rope_select()
    pt_flat = page_table.reshape(-1)
    samp = slice(SAMP_OFF, SAMP_OFF + DEC_BATCH)
    metas = slice(META_OFF, SAMP_OFF)

    ffn1_up_b, ffn1_down_b = ffn1_up.astype(BF16), ffn1_down.astype(BF16)
    ffn2_up_b, ffn2_down_b = ffn2_up.astype(BF16), ffn2_down.astype(BF16)
    w_pa_b, w_pb_b, w_o_b = w_pa.astype(BF16), w_pb.astype(BF16), w_o.astype(BF16)
    ln_g3 = ln_g.reshape(DEPTH * 3, 1, D_MODEL)
    ln_b3 = ln_b.reshape(DEPTH * 3, 1, D_MODEL)
    state_m4 = state_m[:, :, None, :]
    cache_krope_t = jnp.swapaxes(cache_krope, 2, 3)
    conv_rows = (np.arange(BATCH)[:, None] * SEQ + (SEQ - (CONV_W - 1)) + np.arange(CONV_W - 1)[None, :]).reshape(-1)

    st = {k: [] for k in ("ckv_p", "kr_p", "c_p", "n_p", "m_p", "conv_p", "ckv_s", "kr_s", "c_s", "n_s", "m_s",
                          "conv_s")}
    for l in range(DEPTH):
        wp = _prep_layer(l, w_in, b_in, q_norm_g, kv_norm_g, w_uq, w_uk, w_uv)
        x = _ffn(x, l, ffn1_up_b, ffn1_down_b, 3 * l, ln_g3, ln_b3, x_tail=x_tail if l == 0 else None)
        q, k, vb, ckvn, sg, qk, vm, om, ga, gb = _inproj(x, wp, tabs)

        hn, hn_meta, c_p, n_p, m_p = mlstm_prompt_from_rows(qk, vm, om, sg[:, :2 * A_HEADS], conv_w[l], conv_b[l],
                                                            mh_g[l])
        hn_s, c_s, n_s, m_s = _mlstm_sample(
            l, qk[samp][:, None, :], state_conv, vm[samp].astype(F32)[:, None, :], om[samp][:, None, :],
            sg[samp][:, None, :], state_m4, state_C, state_n,
            conv_w[l], conv_b[l][None], mh_g[l][None])
        hn_tail = _tail_tile(hn_meta, hn_s.reshape(DEC_BATCH, A_VW))

        ob = _attn_prompt(q, k, vb)
        ob_meta = _attn_meta(q, k, vb)
        ob_s = _attn_sample(l, pt_flat, q[samp].astype(F32)[:, None, :], ckvn[samp][:, None, :],
                            sg[samp][:, None, :], wp["wk_ext"], e_rope, wp["w_uv"], cache_ckv, cache_krope_t)
        ob_tail = _tail_tile(ob_meta, ob_s.reshape(DEC_BATCH, B_HEADS * B_DV))

        x = _merge_ffn(x, hn, hn_tail, ob, ob_tail, ga, gb, l, w_pa_b, w_pb_b, w_o_b, ffn2_up_b, ffn2_down_b,
                       3 * l + 1, ln_g3, ln_b3)

        kr_all = sg[:, ROPE_LANE:ROPE_LANE + B_ROPE]
        st["ckv_p"].append(ckvn)
        st["kr_p"].append(kr_all)
        st["c_p"].append(c_p)
        st["n_p"].append(n_p)
        st["m_p"].append(m_p)
        st["conv_p"].append(jnp.take(qk, conv_rows, axis=0).reshape(BATCH, CONV_W - 1, 2 * A_QK))
        st["ckv_s"].append(ckvn[samp][:, None, :])
        st["kr_s"].append(kr_all[samp][:, None, :])
        st["c_s"].append(c_s)
        st["n_s"].append(n_s)
        st["m_s"].append(m_s[:, 0, :])
        st["conv_s"].append(jnp.concatenate([state_conv[l][:, 1:], qk[samp][:, None, :]], axis=1))

    y_prompt = x[:N_REAL].reshape(BATCH, SEQ, D_MODEL)
    y_sample = x[samp].reshape(DEC_BATCH, 1, D_MODEL)
    def seq_order_all(rows, width):
        meta_part = jnp.stack([a[metas] for a in rows]).reshape(DEPTH, BATCH, N_META, width)
        main_part = jnp.stack([a[:N_REAL] for a in rows]).reshape(DEPTH, BATCH, SEQ, width)
        return jnp.concatenate([meta_part, main_part], axis=2)

    ckv_p = seq_order_all(st.pop("ckv_p"), KV_RANK)
    kr_p = seq_order_all(st.pop("kr_p"), B_ROPE)
    s = {k: jnp.stack(v) for k, v in st.items()}
    s["ckv_p"], s["kr_p"] = ckv_p, kr_p
    return (y_prompt, y_sample, s["ckv_p"], s["kr_p"], s["c_p"], s["n_p"], s["m_p"], s["conv_p"],
            s["ckv_s"], s["kr_s"], s["c_s"], s["n_s"], s["m_s"], s["conv_s"])
```

```python
import functools

import numpy as np
import jax
import jax.numpy as jnp
from jax import lax
from jax.experimental import pallas as pl
from jax.experimental.pallas import tpu as pltpu

F32 = jnp.float32
BF16 = jnp.bfloat16

D_MODEL = 1024
BATCH = 8
SEQ = 2048
DEPTH = 4
DEC_BATCH = 32
PAST_LEN = 16384
PAGE_SIZE = 128
N_PAGES = PAST_LEN // PAGE_SIZE
N_META = 16
A_HEADS = 4
A_DK = 128
A_DV = 128
A_QK = A_HEADS * A_DK
A_VW = A_HEADS * A_DV
CONV_W = 4
B_HEADS = 8
B_NOPE = 64
B_ROPE = 32
B_DV = 64
Q_RANK = 384
KV_RANK = 256
ROPE_BASE = 10000.0
ATTN_SCALE = (B_NOPE + B_ROPE) ** -0.5
LOG2E = 1.4426950408889634
Q_SCALE = ATTN_SCALE * LOG2E
D_FF = 2816
DN_ALPHA = (2 * DEPTH) ** 0.25
LN_EPS = 1e-5
RMS_EPS = 1e-6
IN_SIZES = (Q_RANK, KV_RANK, B_ROPE, 2 * A_QK, A_VW, A_VW, A_HEADS, A_HEADS, D_MODEL, D_MODEL)

LANE = 128
HALF_ROPE = B_ROPE // 2
ROPE_LANE = B_NOPE

TM = 512
N_REAL = BATCH * SEQ
META_OFF = N_REAL
SAMP_OFF = META_OFF + BATCH * N_META
TAIL_ROWS = BATCH * N_META + DEC_BATCH
T_PAD = N_REAL + TM
N_TILES = T_PAD // TM
REAL_TILES_PER_SEQ = SEQ // TM
S_ALL = SEQ + N_META

G_CQ, G_CKV, G_SG, G_QK, G_V, G_O, G_GA, G_GB = 0, 384, 640, 768, 1792, 2304, 2816, 3840
N_IN_P = 4864

FF_CHUNKS = 1
FF_C = D_FF // FF_CHUNKS

MLSTM_L = 512
MLSTM_HP = 2
MLSTM_SB = 8
CONV_PAD = 8

ATT_TQ = 512
ATT_TK = 512
PAGES_PER_STEP = 32
KEYS_PER_STEP = PAGES_PER_STEP * PAGE_SIZE

VMEM_LIMIT = 56 * 1024 * 1024


def _cparams(sem):
    return pltpu.CompilerParams(dimension_semantics=sem, vmem_limit_bytes=VMEM_LIMIT)


def _const_spec(shape):
    nd = len(shape)
    return pl.BlockSpec(shape, lambda *_: (0,) * nd, pipeline_mode=pl.Buffered(1))


def _layer_spec(layer, shape):
    nd = len(shape)
    return pl.BlockSpec((None,) + shape, lambda *_: (layer,) + (0,) * nd, pipeline_mode=pl.Buffered(1))


def _layer_norm(z, g, b):
    mu = jnp.mean(z, axis=-1, keepdims=True)
    zc = z - mu
    var = jnp.mean(zc * zc, axis=-1, keepdims=True)
    return zc * lax.rsqrt(var + LN_EPS) * g + b


def _rms_norm(z, g):
    return z * lax.rsqrt(jnp.mean(z * z, axis=-1, keepdims=True) + RMS_EPS) * g


def _dot(a, b):
    return jnp.dot(a, b, preferred_element_type=F32)


def _dot_nt(a, b):
    return lax.dot_general(a, b, (((1,), (1,)), ((), ())), preferred_element_type=F32)


def _dot_tn(a, b):
    return lax.dot_general(a, b, (((0,), (0,)), ((), ())), preferred_element_type=F32)


def _log_sigmoid(x):
    return jnp.minimum(x, 0.0) - jnp.log1p(jnp.exp(-jnp.abs(x)))


def _swiglu_postnorm(x, wup_ref, wdn_ref, g, b):
    xb = x.astype(BF16)
    y = jnp.zeros_like(x)
    for c in range(FF_CHUNKS):
        lo = c * FF_C
        a = _dot(xb, wup_ref[:, lo:lo + FF_C])
        u = _dot(xb, wup_ref[:, D_FF + lo:D_FF + lo + FF_C])
        h = (a * jax.nn.sigmoid(a) * u).astype(BF16)
        y = y + _dot(h, wdn_ref[lo:lo + FF_C, :])
    return _layer_norm(DN_ALPHA * x + 0.5 * y, g, b)


def _ffn_kernel(x_ref, wup_ref, wdn_ref, g_ref, b_ref, o_ref):
    o_ref[...] = _swiglu_postnorm(x_ref[...], wup_ref, wdn_ref, g_ref[...], b_ref[...])


def _ffn_split_kernel(xm_ref, xt_ref, wup_ref, wdn_ref, g_ref, b_ref, o_ref):
    x = jnp.where(pl.program_id(0) == N_TILES - 1, xt_ref[...], xm_ref[...])
    o_ref[...] = _swiglu_postnorm(x, wup_ref, wdn_ref, g_ref[...], b_ref[...])


def _ffn(x, layer, w_up, w_down, ln_idx, g, b, x_tail=None):
    row = pl.BlockSpec((TM, D_MODEL), lambda i: (i, 0))
    weights = [_layer_spec(layer, (D_MODEL, 2 * D_FF)), _layer_spec(layer, (D_FF, D_MODEL)),
               _layer_spec(ln_idx, (1, D_MODEL)), _layer_spec(ln_idx, (1, D_MODEL))]
    if x_tail is None:
        body, x_specs, xs = _ffn_kernel, [row], (x,)
    else:
        body, xs = _ffn_split_kernel, (x, x_tail)
        x_specs = [pl.BlockSpec((TM, D_MODEL), lambda i: (jnp.minimum(i, N_REAL // TM - 1), 0)),
                   pl.BlockSpec((TM, D_MODEL), lambda i: (0, 0))]
    return pl.pallas_call(
        body,
        grid=(N_TILES,),
        in_specs=x_specs + weights,
        out_specs=row,
        out_shape=jax.ShapeDtypeStruct((T_PAD, D_MODEL), F32),
        compiler_params=_cparams(("parallel",)),
        name="ffn_postnorm",
    )(*xs, w_up, w_down, g, b)


def _rope(z, tc, ts1, ts2):
    return z * tc + pltpu.roll(z, LANE - HALF_ROPE, 1) * ts1 + pltpu.roll(z, HALF_ROPE, 1) * ts2


def _inproj_kernel(x_ref, w_ref, b_ref, qg_ref, kvg_ref, wuq_ref, wkf_ref, wuv_ref, tc_ref, ts1_ref, ts2_ref,
                   q_ref, k_ref, vb_ref, ckv_ref, sg_ref, qk_ref, vm_ref, om_ref, ga_ref, gb_ref):
    xb = x_ref[...].astype(BF16)

    def proj(off, n):
        return _dot(xb, w_ref[:, off:off + n]) + b_ref[:, off:off + n]

    tc, ts1, ts2 = tc_ref[...], ts1_ref[...], ts2_ref[...]
    cqn = _rms_norm(proj(G_CQ, Q_RANK), qg_ref[...]).astype(BF16)
    q = _dot(cqn, wuq_ref[...])
    for h in range(B_HEADS):
        qh = _rope(q[:, h * LANE:(h + 1) * LANE], tc, ts1, ts2)
        q_ref[:, h * LANE:(h + 1) * LANE] = (qh * Q_SCALE).astype(BF16)
    ckvn = _rms_norm(proj(G_CKV, KV_RANK), kvg_ref[...])
    ckv_ref[...] = ckvn
    sg = _rope(proj(G_SG, LANE), tc, ts1, ts2)
    lane = lax.broadcasted_iota(jnp.int32, sg.shape, 1)
    sg = jnp.where((lane >= A_HEADS) & (lane < 2 * A_HEADS), _log_sigmoid(sg), sg)
    sg_ref[...] = sg
    kin = jnp.concatenate([ckvn.astype(BF16), sg.astype(BF16)], axis=1)
    k_ref[...] = _dot(kin, wkf_ref[...]).astype(BF16)
    vb_ref[...] = _dot(kin[:, :KV_RANK], wuv_ref[...]).astype(BF16)
    qk_ref[...] = proj(G_QK, 2 * A_QK)
    vm_ref[...] = proj(G_V, A_VW).astype(BF16)
    om_ref[...] = proj(G_O, A_VW)
    ga_ref[...] = jax.nn.sigmoid(proj(G_GA, D_MODEL)).astype(BF16)
    gb_ref[...] = jax.nn.sigmoid(proj(G_GB, D_MODEL)).astype(BF16)


def _inproj(x, wp, tabs):
    def row(n):
        return pl.BlockSpec((TM, n), lambda i: (i, 0))

    tab = pl.BlockSpec((TM, LANE), lambda i: (jnp.where(i < N_REAL // TM, i % REAL_TILES_PER_SEQ,
                                                       REAL_TILES_PER_SEQ), 0))
    widths = (B_HEADS * LANE, B_HEADS * LANE, B_HEADS * B_DV, KV_RANK, LANE, 2 * A_QK, A_VW, A_VW, D_MODEL, D_MODEL)
    dtypes = (BF16, BF16, BF16, F32, F32, F32, BF16, F32, BF16, BF16)
    return pl.pallas_call(
        _inproj_kernel,
        grid=(N_TILES,),
        in_specs=[row(D_MODEL), _const_spec((D_MODEL, N_IN_P)), _const_spec((1, N_IN_P)),
                  _const_spec((1, Q_RANK)), _const_spec((1, KV_RANK)),
                  _const_spec((Q_RANK, B_HEADS * LANE)), _const_spec((KV_RANK + LANE, B_HEADS * LANE)),
                  _const_spec((KV_RANK, B_HEADS * B_DV)), tab, tab, tab],
        out_specs=[row(n) for n in widths],
        out_shape=[jax.ShapeDtypeStruct((T_PAD, n), d) for n, d in zip(widths, dtypes)],
        compiler_params=_cparams(("parallel",)),
        name="in_proj",
    )(x, wp["w_in"], wp["b_in"], wp["qg"], wp["kvg"], wp["w_uq"], wp["w_kf"], wp["w_uv"], *tabs)


def _mlstm_prompt_kernel(qm_ref, km_ref, q0_ref, k0_ref, vm_ref, v0_ref, om_ref, o0_ref, gr_ref, gc_ref,
                         cwq_ref, cwk_ref, cbq_ref, cbk_ref, mhg_ref,
                         hm_ref, h0_ref, cx_ref, m_ref, uq_ref, uk_ref, vx_ref):
    for u_ref, a0_ref, am_ref in ((uq_ref, q0_ref, qm_ref), (uk_ref, k0_ref, km_ref)):
        u_ref[0:CONV_PAD, :] = jnp.zeros((CONV_PAD, MLSTM_HP * LANE), F32)
        u_ref[CONV_PAD:CONV_PAD + N_META, :] = a0_ref[...]
        u_ref[CONV_PAD + N_META:, :] = am_ref[...]
    for hh in range(MLSTM_HP):
        lanes = slice(hh * LANE, (hh + 1) * LANE)
        vx_ref[hh, 0:N_META, 0:LANE] = v0_ref[:, lanes]
        vx_ref[hh, N_META:, 0:LANE] = vm_ref[:, lanes]
        vx_ref[hh, :, LANE:] = jnp.ones((S_ALL, LANE), BF16)

    def conv_silu(u_ref, w_ref, b_ref, t0, length):
        acc = b_ref[...]
        for j in range(CONV_W):
            lo = CONV_PAD - (CONV_W - 1) + j + t0
            acc = acc + w_ref[j:j + 1, :] * u_ref[lo:lo + length, :]
        return acc * jax.nn.sigmoid(acc)

    def head_chunk(hh, q, k, t0, length, o_ref, h_ref, r0, cx, m):
        lanes = slice(hh * LANE, (hh + 1) * LANE)
        qb, kb = q.astype(BF16), k.astype(BF16)
        ig_r = gr_ref[hh, 0:1, t0:t0 + length]
        lf_r = gr_ref[hh, 1:2, t0:t0 + length]
        ig_c = gc_ref[hh, t0:t0 + length, 0:1]
        lf_c = gc_ref[hh, t0:t0 + length, 1:2]
        row = lax.broadcasted_iota(jnp.int32, (length, length), 0)
        col = lax.broadcasted_iota(jnp.int32, (length, length), 1)
        tri = row >= col
        b_c = jnp.sum(jnp.where(tri, lf_r, 0.0), axis=1, keepdims=True)
        b_r = jnp.sum(jnp.where(row <= col, lf_c, 0.0), axis=0, keepdims=True)
        dmat = jnp.where(tri, b_c - b_r + ig_r, -jnp.inf)
        inter = b_c + m
        m_row = jnp.maximum(inter, jnp.max(dmat, axis=1, keepdims=True))
        s = _dot_nt(qb, kb) * jnp.exp(dmat - m_row)
        w_prev = jnp.exp(inter - m_row)
        vx = vx_ref[hh, t0:t0 + length, :]
        tot = w_prev * _dot(qb, cx.astype(BF16)) + _dot(s.astype(BF16), vx)
        h = tot[:, :A_DV] / jnp.maximum(jnp.abs(tot[:, A_DV:]), jnp.exp(-m_row))
        b_last = b_c[length - 1:length, :]
        g = b_last - b_c + ig_c
        m_new = jnp.maximum(b_last + m, jnp.max(g, axis=0, keepdims=True))
        decay = jnp.exp(b_last + m - m_new)
        wk = (jnp.exp(g - m_new) * k).astype(BF16)
        cx_new = decay * cx + _dot_tn(wk, vx)
        mu = jnp.mean(h, axis=-1, keepdims=True)
        hc = h - mu
        var = jnp.mean(hc * hc, axis=-1, keepdims=True)
        hn = hc * lax.rsqrt(var + LN_EPS) * mhg_ref[:, lanes] * jax.nn.sigmoid(o_ref[r0:r0 + length, lanes])
        h_ref[r0:r0 + length, lanes] = hn.astype(BF16)
        return cx_new, m_new

    def chunk(t0, length, o_ref, h_ref, r0, states):
        q2 = conv_silu(uq_ref, cwq_ref, cbq_ref, t0, length)
        k2 = conv_silu(uk_ref, cwk_ref, cbk_ref, t0, length) * (A_DK ** -0.5)
        return [head_chunk(hh, q2[:, hh * LANE:(hh + 1) * LANE], k2[:, hh * LANE:(hh + 1) * LANE],
                           t0, length, o_ref, h_ref, r0, *states[hh]) for hh in range(MLSTM_HP)]

    states = [(jnp.zeros((A_DK, 2 * LANE), F32), jnp.zeros((1, 1), F32)) for _ in range(MLSTM_HP)]
    states = chunk(0, N_META, o0_ref, h0_ref, 0, states)
    for c in range(SEQ // MLSTM_L):
        states = chunk(N_META + c * MLSTM_L, MLSTM_L, om_ref, hm_ref, c * MLSTM_L, states)
    for hh, (cx, m) in enumerate(states):
        cx_ref[hh] = cx
        m_ref[hh] = jnp.broadcast_to(m, (1, LANE))


def _mlstm_prompt(qk, vm, om, gr, gc, conv_w, conv_b, mh_g):
    meta_blk = META_OFF // N_META
    width = MLSTM_HP * LANE
    k_off = A_HEADS // MLSTM_HP

    def main(off):
        return pl.BlockSpec((SEQ, width), lambda b, j: (b, j + off))

    def meta(off):
        return pl.BlockSpec((N_META, width), lambda b, j: (meta_blk + b, j + off))

    def wcol(rows, off):
        return pl.BlockSpec((rows, width), lambda b, j: (0, j + off))

    def per_head(*shape):
        nd = len(shape)
        return pl.BlockSpec((None, MLSTM_HP) + shape, lambda b, j: (b, j) + (0,) * nd)

    return pl.pallas_call(
        _mlstm_prompt_kernel,
        grid=(BATCH, A_HEADS // MLSTM_HP),
        in_specs=[main(0), main(k_off), meta(0), meta(k_off), main(0), meta(0), main(0), meta(0),
                  per_head(2, S_ALL), per_head(S_ALL, 2),
                  wcol(CONV_W, 0), wcol(CONV_W, k_off), wcol(1, 0), wcol(1, k_off), wcol(1, 0)],
        out_specs=[main(0),
                   pl.BlockSpec((N_META, width), lambda b, j: (b, j)),
                   per_head(A_DK, 2 * LANE), per_head(1, LANE)],
        out_shape=[jax.ShapeDtypeStruct((N_REAL, A_VW), BF16),
                   jax.ShapeDtypeStruct((BATCH * N_META, A_VW), BF16),
                   jax.ShapeDtypeStruct((BATCH, A_HEADS, A_DK, 2 * LANE), F32),
                   jax.ShapeDtypeStruct((BATCH, A_HEADS, 1, LANE), F32)],
        scratch_shapes=[pltpu.VMEM((CONV_PAD + S_ALL, width), F32), pltpu.VMEM((CONV_PAD + S_ALL, width), F32),
                        pltpu.VMEM((MLSTM_HP, S_ALL, 2 * LANE), BF16)],
        compiler_params=_cparams(("parallel", "parallel")),
        name="mlstm_prompt",
    )(qk, qk, qk, qk, vm, vm, om, om, gr, gc, conv_w, conv_w, conv_b, conv_b, mh_g)


def mlstm_prompt_from_rows(qk, vm, om, gates, conv_w, conv_b, mh_g):
    g = jnp.concatenate([gates[META_OFF:SAMP_OFF].reshape(BATCH, N_META, 2, A_HEADS),
                         gates[:N_REAL].reshape(BATCH, SEQ, 2, A_HEADS)], axis=1)
    gr = jnp.transpose(g, (0, 3, 2, 1))
    gc = jnp.transpose(g, (0, 3, 1, 2))
    hn, hn_meta, cx, m = _mlstm_prompt(qk, vm, om, gr, gc, conv_w, conv_b[None], mh_g[None])
    return hn, hn_meta, cx[..., :A_DV], cx[..., A_DV], m[:, :, 0, 0]


def _mlstm_sample_kernel(qk_ref, cprev_ref, v_ref, o_ref, sg_ref, m_ref, c_ref, n_ref, cw_ref, cb_ref, mhg_ref,
                         hn_ref, cnew_ref, nnew_ref, mnew_ref):
    for i in range(MLSTM_SB):
        _mlstm_sample_row(i, qk_ref, cprev_ref, v_ref, o_ref, sg_ref, m_ref, c_ref, n_ref, cw_ref, cb_ref, mhg_ref,
                          hn_ref, cnew_ref, nnew_ref, mnew_ref)


def _mlstm_sample_row(i, qk_ref, cprev_ref, v_ref, o_ref, sg_ref, m_ref, c_ref, n_ref, cw_ref, cb_ref, mhg_ref,
                      hn_ref, cnew_ref, nnew_ref, mnew_ref):
    acc = (cb_ref[...] + jnp.sum(cw_ref[0:CONV_W - 1, :] * cprev_ref[i], axis=0, keepdims=True)
           + cw_ref[CONV_W - 1:CONV_W, :] * qk_ref[i])
    a = acc * jax.nn.sigmoid(acc)
    sg = sg_ref[i]
    m_in = m_ref[i]
    row8 = lax.broadcasted_iota(jnp.int32, (8, LANE), 0)
    for h in range(A_HEADS):
        q = a[:, h * A_DK:(h + 1) * A_DK]
        k = a[:, A_QK + h * A_DK:A_QK + (h + 1) * A_DK] * (A_DK ** -0.5)
        v = v_ref[i, :, h * A_DV:(h + 1) * A_DV]
        ig = sg[:, h:h + 1]
        lf = sg[:, A_HEADS + h:A_HEADS + h + 1]
        m = m_in[:, h:h + 1]
        c = c_ref[i, h]
        n = n_ref[i, h:h + 1, :]
        inter = lf + m
        m_row = jnp.maximum(inter, ig)
        s = jnp.sum(q * k, axis=-1, keepdims=True) * jnp.exp(ig - m_row)
        w_prev = jnp.exp(inter - m_row)
        q8 = jnp.broadcast_to(q, (8, A_DK)).astype(BF16)
        qc = _dot(q8, c.astype(BF16))[0:1, :]
        num = w_prev * qc + s * v
        den = w_prev * jnp.sum(q * n, axis=-1, keepdims=True) + s
        hh = num / jnp.maximum(jnp.abs(den), jnp.exp(-m_row))
        wk = jnp.exp(ig - m_row) * k
        wk8 = jnp.where(row8 == 0, jnp.broadcast_to(wk, (8, A_DK)), 0.0).astype(BF16)
        v8 = jnp.broadcast_to(v, (8, A_DV)).astype(BF16)
        cnew_ref[i, h] = w_prev * c + _dot_tn(wk8, v8)
        nnew_ref[i, h:h + 1, :] = w_prev * n + wk
        mnew_ref[i, :, h:h + 1] = m_row
        mu = jnp.mean(hh, axis=-1, keepdims=True)
        hc = hh - mu
        var = jnp.mean(hc * hc, axis=-1, keepdims=True)
        hn = hc * lax.rsqrt(var + LN_EPS) * mhg_ref[:, h * A_DV:(h + 1) * A_DV]
        hn_ref[i, :, h * A_DV:(h + 1) * A_DV] = hn * jax.nn.sigmoid(o_ref[i, :, h * A_DV:(h + 1) * A_DV])


def _mlstm_sample(layer, qk_s, conv_prev, v_s, o_s, sg_s, m_s, state_c, state_n, conv_w, conv_b, mh_g):
    def per_b(*shape):
        nd = len(shape)
        return pl.BlockSpec((MLSTM_SB,) + shape, lambda b: (b,) + (0,) * nd)

    def per_lb(*shape):
        nd = len(shape)
        return pl.BlockSpec((None, MLSTM_SB) + shape, lambda b: (layer, b) + (0,) * nd)

    return pl.pallas_call(
        _mlstm_sample_kernel,
        grid=(DEC_BATCH // MLSTM_SB,),
        in_specs=[per_b(1, 2 * A_QK), per_lb(CONV_W - 1, 2 * A_QK), per_b(1, A_VW), per_b(1, A_VW), per_b(1, LANE),
                  per_lb(1, A_HEADS), per_lb(A_HEADS, A_DK, A_DV), per_lb(A_HEADS, A_DK),
                  _const_spec((CONV_W, 2 * A_QK)), _const_spec((1, 2 * A_QK)), _const_spec((1, A_VW))],
        out_specs=[per_b(1, A_VW), per_b(A_HEADS, A_DK, A_DV), per_b(A_HEADS, A_DK), per_b(1, A_HEADS)],
        out_shape=[jax.ShapeDtypeStruct((DEC_BATCH, 1, A_VW), F32),
                   jax.ShapeDtypeStruct((DEC_BATCH, A_HEADS, A_DK, A_DV), F32),
                   jax.ShapeDtypeStruct((DEC_BATCH, A_HEADS, A_DK), F32),
                   jax.ShapeDtypeStruct((DEC_BATCH, 1, A_HEADS), F32)],
        compiler_params=_cparams(("parallel",)),
        name="mlstm_sample",
    )(qk_s, conv_prev, v_s, o_s, sg_s, m_s, state_c, state_n, conv_w, conv_b, mh_g)


def _attn_prompt_kernel(q_ref, k_ref, v_ref, k0_ref, v0_ref, o_ref):
    i = pl.program_id(2)
    sub = ATT_TQ // ATT_TK
    head_lanes = [slice(hh * LANE, (hh + 1) * LANE) for hh in range(2)]

    def head_tile(lanes, state, r0, diag):
        m, l, acc = state
        s = _dot_nt(q_ref[:, lanes], k_ref[pl.ds(r0, ATT_TK), lanes])
        if diag is not None:
            row = lax.broadcasted_iota(jnp.int32, (ATT_TQ, ATT_TK), 0)
            col = lax.broadcasted_iota(jnp.int32, (ATT_TQ, ATT_TK), 1)
            s = jnp.where(col + diag * ATT_TK <= row, s, -jnp.inf)
        m_new = jnp.maximum(m, jnp.max(s, axis=-1, keepdims=True))
        alpha = jnp.exp2(m - m_new)
        p = jnp.exp2(s - m_new)
        l = alpha * l + jnp.sum(p, axis=-1, keepdims=True)
        acc = alpha * acc + _dot(p.astype(BF16), v_ref[pl.ds(r0, ATT_TK), :])
        return m_new, l, acc

    def kv_tile(kt, carry, diag):
        r0 = pl.multiple_of(kt * ATT_TK, ATT_TK)
        return tuple(head_tile(lanes, st, r0, diag) for lanes, st in zip(head_lanes, carry))

    init = []
    for lanes in head_lanes:
        s0 = _dot_nt(q_ref[:, lanes], k0_ref[:, lanes])
        m = jnp.max(s0, axis=-1, keepdims=True)
        p0 = jnp.exp2(s0 - m)
        init.append((m, jnp.sum(p0, axis=-1, keepdims=True), _dot(p0.astype(BF16), v0_ref[...])))
    carry = lax.fori_loop(0, i * sub, functools.partial(kv_tile, diag=None), tuple(init))
    for d in range(sub):
        carry = kv_tile(i * sub + d, carry, d)
    outs = [acc / l for _, l, acc in carry]
    lane = lax.broadcasted_iota(jnp.int32, (ATT_TQ, LANE), 1)
    o_ref[...] = jnp.where(lane < B_DV, outs[0], outs[1]).astype(BF16)


def _attn_prompt(q, k, vb):
    assert ATT_TQ % ATT_TK == 0
    nq = SEQ // ATT_TQ
    meta_blk = META_OFF // N_META
    return pl.pallas_call(
        _attn_prompt_kernel,
        grid=(BATCH, B_HEADS // 2, nq),
        in_specs=[pl.BlockSpec((ATT_TQ, 2 * LANE), lambda b, j, i: (b * nq + i, j)),
                  pl.BlockSpec((SEQ, 2 * LANE), lambda b, j, i: (b, j)),
                  pl.BlockSpec((SEQ, LANE), lambda b, j, i: (b, j)),
                  pl.BlockSpec((N_META, 2 * LANE), lambda b, j, i: (meta_blk + b, j)),
                  pl.BlockSpec((N_META, LANE), lambda b, j, i: (meta_blk + b, j))],
        out_specs=pl.BlockSpec((ATT_TQ, LANE), lambda b, j, i: (b * nq + i, j)),
        out_shape=jax.ShapeDtypeStruct((N_REAL, B_HEADS * B_DV), BF16),
        compiler_params=_cparams(("parallel", "parallel", "parallel")),
        name="attn_prompt",
    )(q, k, vb, k, vb)


def _attn_meta_kernel(q_ref, k_ref, v_ref, o_ref):
    row = lax.broadcasted_iota(jnp.int32, (N_META, N_META), 0)
    col = lax.broadcasted_iota(jnp.int32, (N_META, N_META), 1)
    lane = lax.broadcasted_iota(jnp.int32, (N_META, B_HEADS * B_DV), 1)
    out = jnp.zeros((N_META, B_HEADS * B_DV), F32)
    for h in range(B_HEADS):
        lanes = slice(h * LANE, (h + 1) * LANE)
        s = _dot_nt(q_ref[:, lanes], k_ref[:, lanes])
        s = jnp.where(col <= row, s, -jnp.inf)
        p = jnp.exp2(s - jnp.max(s, axis=-1, keepdims=True))
        l = jnp.sum(p, axis=-1, keepdims=True)
        o = _dot(p.astype(BF16), v_ref[...]) / l
        out = jnp.where(lane // B_DV == h, o, out)
    o_ref[...] = out.astype(BF16)


def _attn_meta(q, k, vb):
    meta_blk = META_OFF // N_META

    def spec(n):
        return pl.BlockSpec((N_META, n), lambda b: (meta_blk + b, 0))

    return pl.pallas_call(
        _attn_meta_kernel,
        grid=(BATCH,),
        in_specs=[spec(B_HEADS * LANE), spec(B_HEADS * LANE), spec(B_HEADS * B_DV)],
        out_specs=pl.BlockSpec((N_META, B_HEADS * B_DV), lambda b: (b, 0)),
        out_shape=jax.ShapeDtypeStruct((BATCH * N_META, B_HEADS * B_DV), BF16),
        compiler_params=_cparams(("parallel",)),
        name="attn_meta",
    )(q, k, vb)


def _attn_sample_kernel(pt_ref, q_ref, ckvn_ref, sg_ref, wk_ref, er_ref, wuv_ref, *rest):
    ck_refs = rest[:PAGES_PER_STEP]
    kr_refs = rest[PAGES_PER_STEP:2 * PAGES_PER_STEP]
    o_ref, qlat_ref, qr_ref, m_ref, l_ref, acc_ref, kb_ref, krb_ref = rest[2 * PAGES_PER_STEP:]
    p_idx = pl.program_id(1)

    @pl.when(p_idx == 0)
    def _init():
        q = q_ref[...]
        row = lax.broadcasted_iota(jnp.int32, (B_HEADS, B_HEADS * LANE), 0)
        lane = lax.broadcasted_iota(jnp.int32, (B_HEADS, B_HEADS * LANE), 1)
        qbd = jnp.where(lane // LANE == row, jnp.broadcast_to(q, (B_HEADS, B_HEADS * LANE)), 0.0).astype(BF16)
        qlat = _dot(qbd, wk_ref[...])
        qr = _dot(qbd, er_ref[...])
        qlat_ref[...] = qlat.astype(BF16)
        qr_ref[...] = qr.astype(BF16)
        ckvn = ckvn_ref[...]
        s_new = (jnp.sum(qlat * ckvn, axis=-1, keepdims=True)
                 + jnp.sum(qr * sg_ref[...], axis=-1, keepdims=True))
        m_ref[...] = s_new
        l_ref[...] = jnp.ones_like(s_new)
        acc_ref[...] = jnp.broadcast_to(ckvn, (B_HEADS, KV_RANK))

    for g in range(PAGES_PER_STEP):
        kb_ref[g * PAGE_SIZE:(g + 1) * PAGE_SIZE, :] = ck_refs[g][...].astype(BF16)
        krb_ref[:, g * PAGE_SIZE:(g + 1) * PAGE_SIZE] = kr_refs[g][...].astype(BF16)
    qr = qr_ref[:, ROPE_LANE:ROPE_LANE + B_ROPE]
    s = _dot_nt(qlat_ref[...], kb_ref[...]) + _dot(qr, krb_ref[...])
    m_old = m_ref[...]
    m_new = jnp.maximum(m_old, jnp.max(s, axis=-1, keepdims=True))
    alpha = jnp.exp2(m_old - m_new)
    p = jnp.exp2(s - m_new)
    l_ref[...] = alpha * l_ref[...] + jnp.sum(p, axis=-1, keepdims=True)
    acc_ref[...] = alpha * acc_ref[...] + _dot(p.astype(BF16), kb_ref[...])
    m_ref[...] = m_new

    @pl.when(p_idx == pl.num_programs(1) - 1)
    def _fin():
        o_lat = (acc_ref[...] / l_ref[...]).astype(BF16)
        o_all = _dot(o_lat, wuv_ref[...])
        row = lax.broadcasted_iota(jnp.int32, o_all.shape, 0)
        lane = lax.broadcasted_iota(jnp.int32, o_all.shape, 1)
        o_ref[...] = jnp.sum(jnp.where(lane // B_DV == row, o_all, 0.0), axis=0, keepdims=True)


def _attn_sample(layer, page_table, q_s, ckvn_s, sg_s, wk_ext, e_rope, w_uv, cache_ckv, cache_krope_t):
    steps = N_PAGES // PAGES_PER_STEP

    def per_b(n):
        return pl.BlockSpec((None, 1, n), lambda b, p, pt: (b, 0, 0))

    def const(shape):
        return pl.BlockSpec(shape, lambda b, p, pt: (0, 0), pipeline_mode=pl.Buffered(1))

    def page(rows, width, g):
        return pl.BlockSpec((None, None, rows, width),
                            lambda b, p, pt: (layer, pt[b * N_PAGES + p * PAGES_PER_STEP + g], 0, 0))

    grid_spec = pltpu.PrefetchScalarGridSpec(
        num_scalar_prefetch=1,
        grid=(DEC_BATCH, steps),
        in_specs=[per_b(B_HEADS * LANE), per_b(KV_RANK), per_b(LANE),
                  const((B_HEADS * LANE, KV_RANK)), const((B_HEADS * LANE, LANE)), const((KV_RANK, B_HEADS * B_DV))]
                 + [page(PAGE_SIZE, KV_RANK, g) for g in range(PAGES_PER_STEP)]
                 + [page(B_ROPE, PAGE_SIZE, g) for g in range(PAGES_PER_STEP)],
        out_specs=pl.BlockSpec((None, 1, B_HEADS * B_DV), lambda b, p, pt: (b, 0, 0)),
        scratch_shapes=[pltpu.VMEM((B_HEADS, KV_RANK), BF16), pltpu.VMEM((B_HEADS, LANE), BF16),
                        pltpu.VMEM((B_HEADS, 1), F32), pltpu.VMEM((B_HEADS, 1), F32),
                        pltpu.VMEM((B_HEADS, KV_RANK), F32),
                        pltpu.VMEM((KEYS_PER_STEP, KV_RANK), BF16), pltpu.VMEM((B_ROPE, KEYS_PER_STEP), BF16)],
    )
    return pl.pallas_call(
        _attn_sample_kernel,
        grid_spec=grid_spec,
        out_shape=jax.ShapeDtypeStruct((DEC_BATCH, 1, B_HEADS * B_DV), F32),
        compiler_params=_cparams(("parallel", "arbitrary")),
        name="attn_sample",
    )(page_table, q_s, ckvn_s, sg_s, wk_ext, e_rope, w_uv,
      *([cache_ckv] * PAGES_PER_STEP), *([cache_krope_t] * PAGES_PER_STEP))


def _merge_ffn_kernel(x_ref, hnm_ref, hnt_ref, obm_ref, obt_ref, ga_ref, gb_ref, wpa_ref, wpb_ref, wo_ref,
                      g1_ref, b1_ref, wup_ref, wdn_ref, g2_ref, b2_ref, o_ref):
    is_tail = pl.program_id(0) == N_TILES - 1
    hn = jnp.where(is_tail, hnt_ref[...], hnm_ref[...])
    ob = jnp.where(is_tail, obt_ref[...], obm_ref[...])
    y_a = _dot(hn, wpa_ref[...])
    y_b = _dot(ob, wpb_ref[...])
    mix = ga_ref[...].astype(F32) * y_a + gb_ref[...].astype(F32) * y_b
    y = _dot(mix.astype(BF16), wo_ref[...])
    x = _layer_norm(DN_ALPHA * x_ref[...] + y, g1_ref[...], b1_ref[...])
    o_ref[...] = _swiglu_postnorm(x, wup_ref, wdn_ref, g2_ref[...], b2_ref[...])


def _merge_ffn(x, hn, hn_tail, ob, ob_tail, ga, gb, layer, w_pa, w_pb, w_o, w_up, w_down, ln_idx, g, b):
    def row(n):
        return pl.BlockSpec((TM, n), lambda i: (i, 0))

    def main(n):
        return pl.BlockSpec((TM, n), lambda i: (jnp.minimum(i, N_REAL // TM - 1), 0))

    def tail(n):
        return pl.BlockSpec((TM, n), lambda i: (0, 0))

    return pl.pallas_call(
        _merge_ffn_kernel,
        grid=(N_TILES,),
        in_specs=[row(D_MODEL), main(A_VW), tail(A_VW), main(B_HEADS * B_DV), tail(B_HEADS * B_DV),
                  row(D_MODEL), row(D_MODEL),
                  _layer_spec(layer, (A_VW, D_MODEL)), _layer_spec(layer, (B_HEADS * B_DV, D_MODEL)),
                  _layer_spec(layer, (D_MODEL, D_MODEL)), _layer_spec(ln_idx, (1, D_MODEL)),
                  _layer_spec(ln_idx, (1, D_MODEL)),
                  _layer_spec(layer, (D_MODEL, 2 * D_FF)), _layer_spec(layer, (D_FF, D_MODEL)),
                  _layer_spec(ln_idx + 1, (1, D_MODEL)), _layer_spec(ln_idx + 1, (1, D_MODEL))],
        out_specs=row(D_MODEL),
        out_shape=jax.ShapeDtypeStruct((T_PAD, D_MODEL), F32),
        compiler_params=_cparams(("parallel",)),
        name="merge_ffn",
    )(x, hn, hn_tail, ob, ob_tail, ga, gb, w_pa, w_pb, w_o, g, b, w_up, w_down, g, b)


def _prep_layer(l, w_in, b_in, q_norm_g, kv_norm_g, w_uq, w_uk, w_uv):
    offs = np.cumsum((0,) + IN_SIZES)

    def grp(a, i):
        return a[..., offs[i]:offs[i + 1]]

    def arrange(a):
        lead = a.shape[:-1]
        z = lambda n: jnp.zeros(lead + (n,), F32)
        sg = jnp.concatenate([grp(a, 6), grp(a, 7), z(ROPE_LANE - 2 * A_HEADS), grp(a, 2),
                              z(LANE - ROPE_LANE - B_ROPE)], axis=-1)
        return jnp.concatenate([grp(a, 0), grp(a, 1), sg, grp(a, 3), grp(a, 4), grp(a, 5), grp(a, 8), grp(a, 9)],
                               axis=-1)

    pad_q = LANE - B_NOPE - B_ROPE
    wuq = jnp.pad(w_uq[l], ((0, 0), (0, 0), (0, pad_q))).reshape(Q_RANK, B_HEADS * LANE)
    wuk = jnp.pad(w_uk[l], ((0, 0), (0, 0), (0, LANE - B_NOPE)))
    place = np.zeros((LANE, B_HEADS, LANE), np.float32)
    for e in range(B_ROPE):
        place[ROPE_LANE + e, :, ROPE_LANE + e] = 1.0
    w_kf = jnp.concatenate([wuk.reshape(KV_RANK, B_HEADS * LANE),
                            jnp.asarray(place).reshape(LANE, B_HEADS * LANE)], axis=0)
    wk_ext = jnp.transpose(wuk, (1, 2, 0)).reshape(B_HEADS * LANE, KV_RANK)
    return {
        "w_in": arrange(w_in[l]).astype(BF16),
        "b_in": arrange(b_in[l])[None, :],
        "qg": q_norm_g[l][None, :],
        "kvg": kv_norm_g[l][None, :],
        "w_uq": wuq.astype(BF16),
        "w_kf": w_kf.astype(BF16),
        "w_uv": w_uv[l].reshape(KV_RANK, B_HEADS * B_DV).astype(BF16),
        "wk_ext": wk_ext.astype(BF16),
    }


def _rope_select():
    e = np.zeros((B_HEADS, LANE, LANE), np.float32)
    for r in range(B_ROPE):
        e[:, ROPE_LANE + r, ROPE_LANE + r] = 1.0
    return jnp.asarray(e.reshape(B_HEADS * LANE, LANE)).astype(BF16)


def _rope_tabs():
    pos_real = N_META + np.arange(SEQ)
    pos_tail = np.zeros((TM,), np.int64)
    pos_tail[:BATCH * N_META] = np.arange(BATCH * N_META) % N_META
    pos_tail[BATCH * N_META:TAIL_ROWS] = PAST_LEN
    pos = jnp.asarray(np.concatenate([pos_real, pos_tail]), dtype=jnp.int32)
    inv = ROPE_BASE ** (-jnp.arange(0, B_ROPE, 2, dtype=F32) / B_ROPE)
    ang = pos.astype(F32)[:, None] * inv[None, :]
    cos, sin = jnp.cos(ang), jnp.sin(ang)
    n = pos.shape[0]
    z = lambda w: jnp.zeros((n, w), F32)
    tail = LANE - ROPE_LANE - B_ROPE
    tc = jnp.concatenate([jnp.ones((n, ROPE_LANE), F32), cos, cos, z(tail)], axis=1)
    ts1 = jnp.concatenate([z(ROPE_LANE), -sin, z(HALF_ROPE), z(tail)], axis=1)
    ts2 = jnp.concatenate([z(ROPE_LANE), z(HALF_ROPE), sin, z(tail)], axis=1)
    return tc, ts1, ts2


def _tail_tile(meta_rows, samp_rows):
    width = meta_rows.shape[1]
    return jnp.concatenate([meta_rows, samp_rows.astype(meta_rows.dtype),
                            jnp.zeros((TM - TAIL_ROWS, width), meta_rows.dtype)], axis=0)


def kernel(x_prompt, x_sample, cache_ckv, cache_krope, page_table, state_C, state_n, state_m, state_conv, meta,
           w_in, b_in, conv_w, conv_b, mh_g, q_norm_g, kv_norm_g, w_uq, w_uk, w_uv, w_pa, w_pb, w_o,
           ffn1_up, ffn1_down, ffn2_up, ffn2_down, ln_g, ln_b):
    x = x_prompt.reshape(N_REAL, D_MODEL)
    x_tail = _tail_tile(jnp.tile(meta.astype(F32), (BATCH, 1)), x_sample.reshape(DEC_BATCH, D_MODEL))
    tabs = _rope_tabs()
    e_rope = _rope_select()
    pt_flat = page_table.reshape(-1)
    samp = slice(SAMP_OFF, SAMP_OFF + DEC_BATCH)
    metas = slice(META_OFF, SAMP_OFF)

    ffn1_up_b, ffn1_down_b = ffn1_up.astype(BF16), ffn1_down.astype(BF16)
    ffn2_up_b, ffn2_down_b = ffn2_up.astype(BF16), ffn2_down.astype(BF16)
    w_pa_b, w_pb_b, w_o_b = w_pa.astype(BF16), w_pb.astype(BF16), w_o.astype(BF16)
    ln_g3 = ln_g.reshape(DEPTH * 3, 1, D_MODEL)
    ln_b3 = ln_b.reshape(DEPTH * 3, 1, D_MODEL)
    state_m4 = state_m[:, :, None, :]
    cache_krope_t = jnp.swapaxes(cache_krope, 2, 3)
    conv_rows = (np.arange(BATCH)[:, None] * SEQ + (SEQ - (CONV_W - 1)) + np.arange(CONV_W - 1)[None, :]).reshape(-1)

    st = {k: [] for k in ("ckv_p", "kr_p", "c_p", "n_p", "m_p", "conv_p", "ckv_s", "kr_s", "c_s", "n_s", "m_s",
                          "conv_s")}
    for l in range(DEPTH):
        wp = _prep_layer(l, w_in, b_in, q_norm_g, kv_norm_g, w_uq, w_uk, w_uv)
        x = _ffn(x, l, ffn1_up_b, ffn1_down_b, 3 * l, ln_g3, ln_b3, x_tail=x_tail if l == 0 else None)
        q, k, vb, ckvn, sg, qk, vm, om, ga, gb = _inproj(x, wp, tabs)

        hn, hn_meta, c_p, n_p, m_p = mlstm_prompt_from_rows(qk, vm, om, sg[:, :2 * A_HEADS], conv_w[l], conv_b[l],
                                                            mh_g[l])
        hn_s, c_s, n_s, m_s = _mlstm_sample(
            l, qk[samp][:, None, :], state_conv, vm[samp].astype(F32)[:, None, :], om[samp][:, None, :],
            sg[samp][:, None, :], state_m4, state_C, state_n,
            conv_w[l], conv_b[l][None], mh_g[l][None])
        hn_tail = _tail_tile(hn_meta, hn_s.reshape(DEC_BATCH, A_VW))

        ob = _attn_prompt(q, k, vb)
        ob_meta = _attn_meta(q, k, vb)
        ob_s = _attn_sample(l, pt_flat, q[samp].astype(F32)[:, None, :], ckvn[samp][:, None, :],
                            sg[samp][:, None, :], wp["wk_ext"], e_rope, wp["w_uv"], cache_ckv, cache_krope_t)
        ob_tail = _tail_tile(ob_meta, ob_s.reshape(DEC_BATCH, B_HEADS * B_DV))

        x = _merge_ffn(x, hn, hn_tail, ob, ob_tail, ga, gb, l, w_pa_b, w_pb_b, w_o_b, ffn2_up_b, ffn2_down_b,
                       3 * l + 1, ln_g3, ln_b3)

        kr_all = sg[:, ROPE_LANE:ROPE_LANE + B_ROPE]
        st["ckv_p"].append(ckvn)
        st["kr_p"].append(kr_all)
        st["c_p"].append(c_p)
        st["n_p"].append(n_p)
        st["m_p"].append(m_p)
        st["conv_p"].append(jnp.take(qk, conv_rows, axis=0).reshape(BATCH, CONV_W - 1, 2 * A_QK))
        st["ckv_s"].append(ckvn[samp][:, None, :])
        st["kr_s"].append(kr_all[samp][:, None, :])
        st["c_s"].append(c_s)
        st["n_s"].append(n_s)
        st["m_s"].append(m_s[:, 0, :])
        st["conv_s"].append(jnp.concatenate([state_conv[l][:, 1:], qk[samp][:, None, :]], axis=1))

    y_prompt = x[:N_REAL].reshape(BATCH, SEQ, D_MODEL)
    y_sample = x[samp].reshape(DEC_BATCH, 1, D_MODEL)
    def seq_order_all(rows, width):
        meta_part = jnp.stack([a[metas] for a in rows]).reshape(DEPTH, BATCH, N_META, width)
        main_part = jnp.stack([a[:N_REAL] for a in rows]).reshape(DEPTH, BATCH, SEQ, width)
        return jnp.concatenate([meta_part, main_part], axis=2)

    ckv_p = seq_order_all(st.pop("ckv_p"), KV_RANK)
    kr_p = seq_order_all(st.pop("kr_p"), B_ROPE)
    s = {k: jnp.stack(v) for k, v in st.items()}
    s["ckv_p"], s["kr_p"] = ckv_p, kr_p
    return (y_prompt, y_sample, s["ckv_p"], s["kr_p"], s["c_p"], s["n_p"], s["m_p"], s["conv_p"],
            s["ckv_s"], s["kr_s"], s["c_s"], s["n_s"], s["m_s"], s["conv_s"])
```

```python
import functools

import numpy as np
import jax
import jax.numpy as jnp
from jax import lax
from jax.experimental import pallas as pl
from jax.experimental.pallas import tpu as pltpu

F32 = jnp.float32
BF16 = jnp.bfloat16

D_MODEL = 1024
BATCH = 8
SEQ = 2048
DEPTH = 4
DEC_BATCH = 32
PAST_LEN = 16384
PAGE_SIZE = 128
N_PAGES = PAST_LEN // PAGE_SIZE
N_META = 16
A_HEADS = 4
A_DK = 128
A_DV = 128
A_QK = A_HEADS * A_DK
A_VW = A_HEADS * A_DV
CONV_W = 4
B_HEADS = 8
B_NOPE = 64
B_ROPE = 32
B_DV = 64
Q_RANK = 384
KV_RANK = 256
ROPE_BASE = 10000.0
ATTN_SCALE = (B_NOPE + B_ROPE) ** -0.5
LOG2E = 1.4426950408889634
Q_SCALE = ATTN_SCALE * LOG2E
D_FF = 2816
DN_ALPHA = (2 * DEPTH) ** 0.25
LN_EPS = 1e-5
RMS_EPS = 1e-6
IN_SIZES = (Q_RANK, KV_RANK, B_ROPE, 2 * A_QK, A_VW, A_VW, A_HEADS, A_HEADS, D_MODEL, D_MODEL)

LANE = 128
HALF_ROPE = B_ROPE // 2
ROPE_LANE = B_NOPE

TM = 512
N_REAL = BATCH * SEQ
META_OFF = N_REAL
SAMP_OFF = META_OFF + BATCH * N_META
TAIL_ROWS = BATCH * N_META + DEC_BATCH
T_PAD = N_REAL + TM
N_TILES = T_PAD // TM
REAL_TILES_PER_SEQ = SEQ // TM
S_ALL = SEQ + N_META

G_CQ, G_CKV, G_SG, G_QK, G_V, G_O, G_GA, G_GB = 0, 384, 640, 768, 1792, 2304, 2816, 3840
N_IN_P = 4864

FF_CHUNKS = 1
FF_C = D_FF // FF_CHUNKS

MLSTM_L = 512
MLSTM_HP = 2
MLSTM_SB = 8
CONV_PAD = 8

ATT_TQ = 512
ATT_TK = 512
PAGES_PER_STEP = 64
KEYS_PER_STEP = PAGES_PER_STEP * PAGE_SIZE

VMEM_LIMIT = 56 * 1024 * 1024


def _cparams(sem):
    return pltpu.CompilerParams(dimension_semantics=sem, vmem_limit_bytes=VMEM_LIMIT)


def _const_spec(shape):
    nd = len(shape)
    return pl.BlockSpec(shape, lambda *_: (0,) * nd, pipeline_mode=pl.Buffered(1))


def _layer_spec(layer, shape):
    nd = len(shape)
    return pl.BlockSpec((None,) + shape, lambda *_: (layer,) + (0,) * nd, pipeline_mode=pl.Buffered(1))


def _layer_norm(z, g, b):
    mu = jnp.mean(z, axis=-1, keepdims=True)
    zc = z - mu
    var = jnp.mean(zc * zc, axis=-1, keepdims=True)
    return zc * lax.rsqrt(var + LN_EPS) * g + b


def _rms_norm(z, g):
    return z * lax.rsqrt(jnp.mean(z * z, axis=-1, keepdims=True) + RMS_EPS) * g


def _dot(a, b):
    return jnp.dot(a, b, preferred_element_type=F32)


def _dot_nt(a, b):
    return lax.dot_general(a, b, (((1,), (1,)), ((), ())), preferred_element_type=F32)


def _dot_tn(a, b):
    return lax.dot_general(a, b, (((0,), (0,)), ((), ())), preferred_element_type=F32)


def _log_sigmoid(x):
    return jnp.minimum(x, 0.0) - jnp.log1p(jnp.exp(-jnp.abs(x)))


def _swiglu_postnorm(x, wup_ref, wdn_ref, g, b):
    xb = x.astype(BF16)
    y = jnp.zeros_like(x)
    for c in range(FF_CHUNKS):
        lo = c * FF_C
        a = _dot(xb, wup_ref[:, lo:lo + FF_C])
        u = _dot(xb, wup_ref[:, D_FF + lo:D_FF + lo + FF_C])
        h = (a * jax.nn.sigmoid(a) * u).astype(BF16)
        y = y + _dot(h, wdn_ref[lo:lo + FF_C, :])
    return _layer_norm(DN_ALPHA * x + 0.5 * y, g, b)


def _ffn_kernel(x_ref, wup_ref, wdn_ref, g_ref, b_ref, o_ref):
    o_ref[...] = _swiglu_postnorm(x_ref[...], wup_ref, wdn_ref, g_ref[...], b_ref[...])


def _ffn_split_kernel(xm_ref, xt_ref, wup_ref, wdn_ref, g_ref, b_ref, o_ref):
    x = jnp.where(pl.program_id(0) == N_TILES - 1, xt_ref[...], xm_ref[...])
    o_ref[...] = _swiglu_postnorm(x, wup_ref, wdn_ref, g_ref[...], b_ref[...])


def _ffn(x, layer, w_up, w_down, ln_idx, g, b, x_tail=None):
    row = pl.BlockSpec((TM, D_MODEL), lambda i: (i, 0))
    weights = [_layer_spec(layer, (D_MODEL, 2 * D_FF)), _layer_spec(layer, (D_FF, D_MODEL)),
               _layer_spec(ln_idx, (1, D_MODEL)), _layer_spec(ln_idx, (1, D_MODEL))]
    if x_tail is None:
        body, x_specs, xs = _ffn_kernel, [row], (x,)
    else:
        body, xs = _ffn_split_kernel, (x, x_tail)
        x_specs = [pl.BlockSpec((TM, D_MODEL), lambda i: (jnp.minimum(i, N_REAL // TM - 1), 0)),
                   pl.BlockSpec((TM, D_MODEL), lambda i: (0, 0))]
    return pl.pallas_call(
        body,
        grid=(N_TILES,),
        in_specs=x_specs + weights,
        out_specs=row,
        out_shape=jax.ShapeDtypeStruct((T_PAD, D_MODEL), F32),
        compiler_params=_cparams(("parallel",)),
        name="ffn_postnorm",
    )(*xs, w_up, w_down, g, b)


def _rope(z, tc, ts1, ts2):
    return z * tc + pltpu.roll(z, LANE - HALF_ROPE, 1) * ts1 + pltpu.roll(z, HALF_ROPE, 1) * ts2


def _inproj_kernel(x_ref, w_ref, b_ref, qg_ref, kvg_ref, wuq_ref, wkf_ref, wuv_ref, tc_ref, ts1_ref, ts2_ref,
                   q_ref, k_ref, vb_ref, ckv_ref, sg_ref, qk_ref, vm_ref, om_ref, ga_ref, gb_ref):
    xb = x_ref[...].astype(BF16)

    def proj(off, n):
        return _dot(xb, w_ref[:, off:off + n]) + b_ref[:, off:off + n]

    tc, ts1, ts2 = tc_ref[...], ts1_ref[...], ts2_ref[...]
    cqn = _rms_norm(proj(G_CQ, Q_RANK), qg_ref[...]).astype(BF16)
    q = _dot(cqn, wuq_ref[...])
    for h in range(B_HEADS):
        qh = _rope(q[:, h * LANE:(h + 1) * LANE], tc, ts1, ts2)
        q_ref[:, h * LANE:(h + 1) * LANE] = (qh * Q_SCALE).astype(BF16)
    ckvn = _rms_norm(proj(G_CKV, KV_RANK), kvg_ref[...])
    ckv_ref[...] = ckvn
    sg = _rope(proj(G_SG, LANE), tc, ts1, ts2)
    lane = lax.broadcasted_iota(jnp.int32, sg.shape, 1)
    sg = jnp.where((lane >= A_HEADS) & (lane < 2 * A_HEADS), _log_sigmoid(sg), sg)
    sg_ref[...] = sg
    kin = jnp.concatenate([ckvn.astype(BF16), sg.astype(BF16)], axis=1)
    k_ref[...] = _dot(kin, wkf_ref[...]).astype(BF16)
    vb_ref[...] = _dot(kin[:, :KV_RANK], wuv_ref[...]).astype(BF16)
    qk_ref[...] = proj(G_QK, 2 * A_QK)
    vm_ref[...] = proj(G_V, A_VW).astype(BF16)
    om_ref[...] = proj(G_O, A_VW)
    ga_ref[...] = jax.nn.sigmoid(proj(G_GA, D_MODEL)).astype(BF16)
    gb_ref[...] = jax.nn.sigmoid(proj(G_GB, D_MODEL)).astype(BF16)


def _inproj(x, wp, tabs):
    def row(n):
        return pl.BlockSpec((TM, n), lambda i: (i, 0))

    tab = pl.BlockSpec((TM, LANE), lambda i: (jnp.where(i < N_REAL // TM, i % REAL_TILES_PER_SEQ,
                                                       REAL_TILES_PER_SEQ), 0))
    widths = (B_HEADS * LANE, B_HEADS * LANE, B_HEADS * B_DV, KV_RANK, LANE, 2 * A_QK, A_VW, A_VW, D_MODEL, D_MODEL)
    dtypes = (BF16, BF16, BF16, F32, F32, F32, BF16, F32, BF16, BF16)
    return pl.pallas_call(
        _inproj_kernel,
        grid=(N_TILES,),
        in_specs=[row(D_MODEL), _const_spec((D_MODEL, N_IN_P)), _const_spec((1, N_IN_P)),
                  _const_spec((1, Q_RANK)), _const_spec((1, KV_RANK)),
                  _const_spec((Q_RANK, B_HEADS * LANE)), _const_spec((KV_RANK + LANE, B_HEADS * LANE)),
                  _const_spec((KV_RANK, B_HEADS * B_DV)), tab, tab, tab],
        out_specs=[row(n) for n in widths],
        out_shape=[jax.ShapeDtypeStruct((T_PAD, n), d) for n, d in zip(widths, dtypes)],
        compiler_params=_cparams(("parallel",)),
        name="in_proj",
    )(x, wp["w_in"], wp["b_in"], wp["qg"], wp["kvg"], wp["w_uq"], wp["w_kf"], wp["w_uv"], *tabs)


def _mlstm_prompt_kernel(qm_ref, km_ref, q0_ref, k0_ref, vm_ref, v0_ref, om_ref, o0_ref, gr_ref, gc_ref,
                         cwq_ref, cwk_ref, cbq_ref, cbk_ref, mhg_ref,
                         hm_ref, h0_ref, cx_ref, m_ref, uq_ref, uk_ref, vx_ref):
    for u_ref, a0_ref, am_ref in ((uq_ref, q0_ref, qm_ref), (uk_ref, k0_ref, km_ref)):
        u_ref[0:CONV_PAD, :] = jnp.zeros((CONV_PAD, MLSTM_HP * LANE), F32)
        u_ref[CONV_PAD:CONV_PAD + N_META, :] = a0_ref[...]
        u_ref[CONV_PAD + N_META:, :] = am_ref[...]
    for hh in range(MLSTM_HP):
        lanes = slice(hh * LANE, (hh + 1) * LANE)
        vx_ref[hh, 0:N_META, 0:LANE] = v0_ref[:, lanes]
        vx_ref[hh, N_META:, 0:LANE] = vm_ref[:, lanes]
        vx_ref[hh, :, LANE:] = jnp.ones((S_ALL, LANE), BF16)

    def conv_silu(u_ref, w_ref, b_ref, t0, length):
        acc = b_ref[...]
        for j in range(CONV_W):
            lo = CONV_PAD - (CONV_W - 1) + j + t0
            acc = acc + w_ref[j:j + 1, :] * u_ref[lo:lo + length, :]
        return acc * jax.nn.sigmoid(acc)

    def head_chunk(hh, q, k, t0, length, o_ref, h_ref, r0, cx, m):
        lanes = slice(hh * LANE, (hh + 1) * LANE)
        qb, kb = q.astype(BF16), k.astype(BF16)
        ig_r = gr_ref[hh, 0:1, t0:t0 + length]
        lf_r = gr_ref[hh, 1:2, t0:t0 + length]
        ig_c = gc_ref[hh, t0:t0 + length, 0:1]
        lf_c = gc_ref[hh, t0:t0 + length, 1:2]
        row = lax.broadcasted_iota(jnp.int32, (length, length), 0)
        col = lax.broadcasted_iota(jnp.int32, (length, length), 1)
        tri = row >= col
        b_c = jnp.sum(jnp.where(tri, lf_r, 0.0), axis=1, keepdims=True)
        b_r = jnp.sum(jnp.where(row <= col, lf_c, 0.0), axis=0, keepdims=True)
        dmat = jnp.where(tri, b_c - b_r + ig_r, -jnp.inf)
        inter = b_c + m
        m_row = jnp.maximum(inter, jnp.max(dmat, axis=1, keepdims=True))
        s = _dot_nt(qb, kb) * jnp.exp(dmat - m_row)
        w_prev = jnp.exp(inter - m_row)
        vx = vx_ref[hh, t0:t0 + length, :]
        tot = w_prev * _dot(qb, cx.astype(BF16)) + _dot(s.astype(BF16), vx)
        h = tot[:, :A_DV] / jnp.maximum(jnp.abs(tot[:, A_DV:]), jnp.exp(-m_row))
        b_last = b_c[length - 1:length, :]
        g = b_last - b_c + ig_c
        m_new = jnp.maximum(b_last + m, jnp.max(g, axis=0, keepdims=True))
        decay = jnp.exp(b_last + m - m_new)
        wk = (jnp.exp(g - m_new) * k).astype(BF16)
        cx_new = decay * cx + _dot_tn(wk, vx)
        mu = jnp.mean(h, axis=-1, keepdims=True)
        hc = h - mu
        var = jnp.mean(hc * hc, axis=-1, keepdims=True)
        hn = hc * lax.rsqrt(var + LN_EPS) * mhg_ref[:, lanes] * jax.nn.sigmoid(o_ref[r0:r0 + length, lanes])
        h_ref[r0:r0 + length, lanes] = hn.astype(BF16)
        return cx_new, m_new

    def chunk(t0, length, o_ref, h_ref, r0, states):
        q2 = conv_silu(uq_ref, cwq_ref, cbq_ref, t0, length)
        k2 = conv_silu(uk_ref, cwk_ref, cbk_ref, t0, length) * (A_DK ** -0.5)
        return [head_chunk(hh, q2[:, hh * LANE:(hh + 1) * LANE], k2[:, hh * LANE:(hh + 1) * LANE],
                           t0, length, o_ref, h_ref, r0, *states[hh]) for hh in range(MLSTM_HP)]

    states = [(jnp.zeros((A_DK, 2 * LANE), F32), jnp.zeros((1, 1), F32)) for _ in range(MLSTM_HP)]
    states = chunk(0, N_META, o0_ref, h0_ref, 0, states)
    for c in range(SEQ // MLSTM_L):
        states = chunk(N_META + c * MLSTM_L, MLSTM_L, om_ref, hm_ref, c * MLSTM_L, states)
    for hh, (cx, m) in enumerate(states):
        cx_ref[hh] = cx
        m_ref[hh] = jnp.broadcast_to(m, (1, LANE))


def _mlstm_prompt(qk, vm, om, gr, gc, conv_w, conv_b, mh_g):
    meta_blk = META_OFF // N_META
    width = MLSTM_HP * LANE
    k_off = A_HEADS // MLSTM_HP

    def main(off):
        return pl.BlockSpec((SEQ, width), lambda b, j: (b, j + off))

    def meta(off):
        return pl.BlockSpec((N_META, width), lambda b, j: (meta_blk + b, j + off))

    def wcol(rows, off):
        return pl.BlockSpec((rows, width), lambda b, j: (0, j + off))

    def per_head(*shape):
        nd = len(shape)
        return pl.BlockSpec((None, MLSTM_HP) + shape, lambda b, j: (b, j) + (0,) * nd)

    return pl.pallas_call(
        _mlstm_prompt_kernel,
        grid=(BATCH, A_HEADS // MLSTM_HP),
        in_specs=[main(0), main(k_off), meta(0), meta(k_off), main(0), meta(0), main(0), meta(0),
                  per_head(2, S_ALL), per_head(S_ALL, 2),
                  wcol(CONV_W, 0), wcol(CONV_W, k_off), wcol(1, 0), wcol(1, k_off), wcol(1, 0)],
        out_specs=[main(0),
                   pl.BlockSpec((N_META, width), lambda b, j: (b, j)),
                   per_head(A_DK, 2 * LANE), per_head(1, LANE)],
        out_shape=[jax.ShapeDtypeStruct((N_REAL, A_VW), BF16),
                   jax.ShapeDtypeStruct((BATCH * N_META, A_VW), BF16),
                   jax.ShapeDtypeStruct((BATCH, A_HEADS, A_DK, 2 * LANE), F32),
                   jax.ShapeDtypeStruct((BATCH, A_HEADS, 1, LANE), F32)],
        scratch_shapes=[pltpu.VMEM((CONV_PAD + S_ALL, width), F32), pltpu.VMEM((CONV_PAD + S_ALL, width), F32),
                        pltpu.VMEM((MLSTM_HP, S_ALL, 2 * LANE), BF16)],
        compiler_params=_cparams(("parallel", "parallel")),
        name="mlstm_prompt",
    )(qk, qk, qk, qk, vm, vm, om, om, gr, gc, conv_w, conv_w, conv_b, conv_b, mh_g)


def mlstm_prompt_from_rows(qk, vm, om, gates, conv_w, conv_b, mh_g):
    g = jnp.concatenate([gates[META_OFF:SAMP_OFF].reshape(BATCH, N_META, 2, A_HEADS),
                         gates[:N_REAL].reshape(BATCH, SEQ, 2, A_HEADS)], axis=1)
    gr = jnp.transpose(g, (0, 3, 2, 1))
    gc = jnp.transpose(g, (0, 3, 1, 2))
    hn, hn_meta, cx, m = _mlstm_prompt(qk, vm, om, gr, gc, conv_w, conv_b[None], mh_g[None])
    return hn, hn_meta, cx[..., :A_DV], cx[..., A_DV], m[:, :, 0, 0]


def _mlstm_sample_kernel(qk_ref, cprev_ref, v_ref, o_ref, sg_ref, m_ref, c_ref, n_ref, cw_ref, cb_ref, mhg_ref,
                         hn_ref, cnew_ref, nnew_ref, mnew_ref):
    for i in range(MLSTM_SB):
        _mlstm_sample_row(i, qk_ref, cprev_ref, v_ref, o_ref, sg_ref, m_ref, c_ref, n_ref, cw_ref, cb_ref, mhg_ref,
                          hn_ref, cnew_ref, nnew_ref, mnew_ref)


def _mlstm_sample_row(i, qk_ref, cprev_ref, v_ref, o_ref, sg_ref, m_ref, c_ref, n_ref, cw_ref, cb_ref, mhg_ref,
                      hn_ref, cnew_ref, nnew_ref, mnew_ref):
    acc = (cb_ref[...] + jnp.sum(cw_ref[0:CONV_W - 1, :] * cprev_ref[i], axis=0, keepdims=True)
           + cw_ref[CONV_W - 1:CONV_W, :] * qk_ref[i])
    a = acc * jax.nn.sigmoid(acc)
    sg = sg_ref[i]
    m_in = m_ref[i]
    row8 = lax.broadcasted_iota(jnp.int32, (8, LANE), 0)
    for h in range(A_HEADS):
        q = a[:, h * A_DK:(h + 1) * A_DK]
        k = a[:, A_QK + h * A_DK:A_QK + (h + 1) * A_DK] * (A_DK ** -0.5)
        v = v_ref[i, :, h * A_DV:(h + 1) * A_DV]
        ig = sg[:, h:h + 1]
        lf = sg[:, A_HEADS + h:A_HEADS + h + 1]
        m = m_in[:, h:h + 1]
        c = c_ref[i, h]
        n = n_ref[i, h:h + 1, :]
        inter = lf + m
        m_row = jnp.maximum(inter, ig)
        s = jnp.sum(q * k, axis=-1, keepdims=True) * jnp.exp(ig - m_row)
        w_prev = jnp.exp(inter - m_row)
        q8 = jnp.broadcast_to(q, (8, A_DK)).astype(BF16)
        qc = _dot(q8, c.astype(BF16))[0:1, :]
        num = w_prev * qc + s * v
        den = w_prev * jnp.sum(q * n, axis=-1, keepdims=True) + s
        hh = num / jnp.maximum(jnp.abs(den), jnp.exp(-m_row))
        wk = jnp.exp(ig - m_row) * k
        wk8 = jnp.where(row8 == 0, jnp.broadcast_to(wk, (8, A_DK)), 0.0).astype(BF16)
        v8 = jnp.broadcast_to(v, (8, A_DV)).astype(BF16)
        cnew_ref[i, h] = w_prev * c + _dot_tn(wk8, v8)
        nnew_ref[i, h:h + 1, :] = w_prev * n + wk
        mnew_ref[i, :, h:h + 1] = m_row
        mu = jnp.mean(hh, axis=-1, keepdims=True)
        hc = hh - mu
        var = jnp.mean(hc * hc, axis=-1, keepdims=True)
        hn = hc * lax.rsqrt(var + LN_EPS) * mhg_ref[:, h * A_DV:(h + 1) * A_DV]
        hn_ref[i, :, h * A_DV:(h + 1) * A_DV] = hn * jax.nn.sigmoid(o_ref[i, :, h * A_DV:(h + 1) * A_DV])


def _mlstm_sample(layer, qk_s, conv_prev, v_s, o_s, sg_s, m_s, state_c, state_n, conv_w, conv_b, mh_g):
    def per_b(*shape):
        nd = len(shape)
        return pl.BlockSpec((MLSTM_SB,) + shape, lambda b: (b,) + (0,) * nd)

    def per_lb(*shape):
        nd = len(shape)
        return pl.BlockSpec((None, MLSTM_SB) + shape, lambda b: (layer, b) + (0,) * nd)

    return pl.pallas_call(
        _mlstm_sample_kernel,
        grid=(DEC_BATCH // MLSTM_SB,),
        in_specs=[per_b(1, 2 * A_QK), per_lb(CONV_W - 1, 2 * A_QK), per_b(1, A_VW), per_b(1, A_VW), per_b(1, LANE),
                  per_lb(1, A_HEADS), per_lb(A_HEADS, A_DK, A_DV), per_lb(A_HEADS, A_DK),
                  _const_spec((CONV_W, 2 * A_QK)), _const_spec((1, 2 * A_QK)), _const_spec((1, A_VW))],
        out_specs=[per_b(1, A_VW), per_b(A_HEADS, A_DK, A_DV), per_b(A_HEADS, A_DK), per_b(1, A_HEADS)],
        out_shape=[jax.ShapeDtypeStruct((DEC_BATCH, 1, A_VW), F32),
                   jax.ShapeDtypeStruct((DEC_BATCH, A_HEADS, A_DK, A_DV), F32),
                   jax.ShapeDtypeStruct((DEC_BATCH, A_HEADS, A_DK), F32),
                   jax.ShapeDtypeStruct((DEC_BATCH, 1, A_HEADS), F32)],
        compiler_params=_cparams(("parallel",)),
        name="mlstm_sample",
    )(qk_s, conv_prev, v_s, o_s, sg_s, m_s, state_c, state_n, conv_w, conv_b, mh_g)


def _attn_prompt_kernel(q_ref, k_ref, v_ref, k0_ref, v0_ref, o_ref):
    i = pl.program_id(2)
    sub = ATT_TQ // ATT_TK
    head_lanes = [slice(hh * LANE, (hh + 1) * LANE) for hh in range(2)]

    def head_tile(lanes, state, r0, diag):
        m, l, acc = state
        s = _dot_nt(q_ref[:, lanes], k_ref[pl.ds(r0, ATT_TK), lanes])
        if diag is not None:
            row = lax.broadcasted_iota(jnp.int32, (ATT_TQ, ATT_TK), 0)
            col = lax.broadcasted_iota(jnp.int32, (ATT_TQ, ATT_TK), 1)
            s = jnp.where(col + diag * ATT_TK <= row, s, -jnp.inf)
        m_new = jnp.maximum(m, jnp.max(s, axis=-1, keepdims=True))
        alpha = jnp.exp2(m - m_new)
        p = jnp.exp2(s - m_new)
        l = alpha * l + jnp.sum(p, axis=-1, keepdims=True)
        acc = alpha * acc + _dot(p.astype(BF16), v_ref[pl.ds(r0, ATT_TK), :])
        return m_new, l, acc

    def kv_tile(kt, carry, diag):
        r0 = pl.multiple_of(kt * ATT_TK, ATT_TK)
        return tuple(head_tile(lanes, st, r0, diag) for lanes, st in zip(head_lanes, carry))

    init = []
    for lanes in head_lanes:
        s0 = _dot_nt(q_ref[:, lanes], k0_ref[:, lanes])
        m = jnp.max(s0, axis=-1, keepdims=True)
        p0 = jnp.exp2(s0 - m)
        init.append((m, jnp.sum(p0, axis=-1, keepdims=True), _dot(p0.astype(BF16), v0_ref[...])))
    carry = lax.fori_loop(0, i * sub, functools.partial(kv_tile, diag=None), tuple(init))
    for d in range(sub):
        carry = kv_tile(i * sub + d, carry, d)
    outs = [acc / l for _, l, acc in carry]
    lane = lax.broadcasted_iota(jnp.int32, (ATT_TQ, LANE), 1)
    o_ref[...] = jnp.where(lane < B_DV, outs[0], outs[1]).astype(BF16)


def _attn_prompt(q, k, vb):
    assert ATT_TQ % ATT_TK == 0
    nq = SEQ // ATT_TQ
    meta_blk = META_OFF // N_META
    return pl.pallas_call(
        _attn_prompt_kernel,
        grid=(BATCH, B_HEADS // 2, nq),
        in_specs=[pl.BlockSpec((ATT_TQ, 2 * LANE), lambda b, j, i: (b * nq + i, j)),
                  pl.BlockSpec((SEQ, 2 * LANE), lambda b, j, i: (b, j)),
                  pl.BlockSpec((SEQ, LANE), lambda b, j, i: (b, j)),
                  pl.BlockSpec((N_META, 2 * LANE), lambda b, j, i: (meta_blk + b, j)),
                  pl.BlockSpec((N_META, LANE), lambda b, j, i: (meta_blk + b, j))],
        out_specs=pl.BlockSpec((ATT_TQ, LANE), lambda b, j, i: (b * nq + i, j)),
        out_shape=jax.ShapeDtypeStruct((N_REAL, B_HEADS * B_DV), BF16),
        compiler_params=_cparams(("parallel", "parallel", "parallel")),
        name="attn_prompt",
    )(q, k, vb, k, vb)


def _attn_meta_kernel(q_ref, k_ref, v_ref, o_ref):
    row = lax.broadcasted_iota(jnp.int32, (N_META, N_META), 0)
    col = lax.broadcasted_iota(jnp.int32, (N_META, N_META), 1)
    lane = lax.broadcasted_iota(jnp.int32, (N_META, B_HEADS * B_DV), 1)
    out = jnp.zeros((N_META, B_HEADS * B_DV), F32)
    for h in range(B_HEADS):
        lanes = slice(h * LANE, (h + 1) * LANE)
        s = _dot_nt(q_ref[:, lanes], k_ref[:, lanes])
        s = jnp.where(col <= row, s, -jnp.inf)
        p = jnp.exp2(s - jnp.max(s, axis=-1, keepdims=True))
        l = jnp.sum(p, axis=-1, keepdims=True)
        o = _dot(p.astype(BF16), v_ref[...]) / l
        out = jnp.where(lane // B_DV == h, o, out)
    o_ref[...] = out.astype(BF16)


def _attn_meta(q, k, vb):
    meta_blk = META_OFF // N_META

    def spec(n):
        return pl.BlockSpec((N_META, n), lambda b: (meta_blk + b, 0))

    return pl.pallas_call(
        _attn_meta_kernel,
        grid=(BATCH,),
        in_specs=[spec(B_HEADS * LANE), spec(B_HEADS * LANE), spec(B_HEADS * B_DV)],
        out_specs=pl.BlockSpec((N_META, B_HEADS * B_DV), lambda b: (b, 0)),
        out_shape=jax.ShapeDtypeStruct((BATCH * N_META, B_HEADS * B_DV), BF16),
        compiler_params=_cparams(("parallel",)),
        name="attn_meta",
    )(q, k, vb)


def _attn_sample_kernel(pt_ref, q_ref, ckvn_ref, sg_ref, wk_ref, er_ref, wuv_ref, *rest):
    ck_refs = rest[:PAGES_PER_STEP]
    kr_refs = rest[PAGES_PER_STEP:2 * PAGES_PER_STEP]
    o_ref, qlat_ref, qr_ref, m_ref, l_ref, acc_ref, kb_ref, krb_ref = rest[2 * PAGES_PER_STEP:]
    p_idx = pl.program_id(1)

    @pl.when(p_idx == 0)
    def _init():
        q = q_ref[...]
        row = lax.broadcasted_iota(jnp.int32, (B_HEADS, B_HEADS * LANE), 0)
        lane = lax.broadcasted_iota(jnp.int32, (B_HEADS, B_HEADS * LANE), 1)
        qbd = jnp.where(lane // LANE == row, jnp.broadcast_to(q, (B_HEADS, B_HEADS * LANE)), 0.0).astype(BF16)
        qlat = _dot(qbd, wk_ref[...])
        qr = _dot(qbd, er_ref[...])
        qlat_ref[...] = qlat.astype(BF16)
        qr_ref[...] = qr.astype(BF16)
        ckvn = ckvn_ref[...]
        s_new = (jnp.sum(qlat * ckvn, axis=-1, keepdims=True)
                 + jnp.sum(qr * sg_ref[...], axis=-1, keepdims=True))
        m_ref[...] = s_new
        l_ref[...] = jnp.ones_like(s_new)
        acc_ref[...] = jnp.broadcast_to(ckvn, (B_HEADS, KV_RANK))

    for g in range(PAGES_PER_STEP):
        kb_ref[g * PAGE_SIZE:(g + 1) * PAGE_SIZE, :] = ck_refs[g][...].astype(BF16)
        krb_ref[:, g * PAGE_SIZE:(g + 1) * PAGE_SIZE] = kr_refs[g][...].astype(BF16)
    qr = qr_ref[:, ROPE_LANE:ROPE_LANE + B_ROPE]
    s = _dot_nt(qlat_ref[...], kb_ref[...]) + _dot(qr, krb_ref[...])
    m_old = m_ref[...]
    m_new = jnp.maximum(m_old, jnp.max(s, axis=-1, keepdims=True))
    alpha = jnp.exp2(m_old - m_new)
    p = jnp.exp2(s - m_new)
    l_ref[...] = alpha * l_ref[...] + jnp.sum(p, axis=-1, keepdims=True)
    acc_ref[...] = alpha * acc_ref[...] + _dot(p.astype(BF16), kb_ref[...])
    m_ref[...] = m_new

    @pl.when(p_idx == pl.num_programs(1) - 1)
    def _fin():
        o_lat = (acc_ref[...] / l_ref[...]).astype(BF16)
        o_all = _dot(o_lat, wuv_ref[...])
        row = lax.broadcasted_iota(jnp.int32, o_all.shape, 0)
        lane = lax.broadcasted_iota(jnp.int32, o_all.shape, 1)
        o_ref[...] = jnp.sum(jnp.where(lane // B_DV == row, o_all, 0.0), axis=0, keepdims=True)


def _attn_sample(layer, page_table, q_s, ckvn_s, sg_s, wk_ext, e_rope, w_uv, cache_ckv, cache_krope_t):
    steps = N_PAGES // PAGES_PER_STEP

    def per_b(n):
        return pl.BlockSpec((None, 1, n), lambda b, p, pt: (b, 0, 0))

    def const(shape):
        return pl.BlockSpec(shape, lambda b, p, pt: (0, 0), pipeline_mode=pl.Buffered(1))

    def page(rows, width, g):
        return pl.BlockSpec((None, None, rows, width),
                            lambda b, p, pt: (layer, pt[b * N_PAGES + p * PAGES_PER_STEP + g], 0, 0))

    grid_spec = pltpu.PrefetchScalarGridSpec(
        num_scalar_prefetch=1,
        grid=(DEC_BATCH, steps),
        in_specs=[per_b(B_HEADS * LANE), per_b(KV_RANK), per_b(LANE),
                  const((B_HEADS * LANE, KV_RANK)), const((B_HEADS * LANE, LANE)), const((KV_RANK, B_HEADS * B_DV))]
                 + [page(PAGE_SIZE, KV_RANK, g) for g in range(PAGES_PER_STEP)]
                 + [page(B_ROPE, PAGE_SIZE, g) for g in range(PAGES_PER_STEP)],
        out_specs=pl.BlockSpec((None, 1, B_HEADS * B_DV), lambda b, p, pt: (b, 0, 0)),
        scratch_shapes=[pltpu.VMEM((B_HEADS, KV_RANK), BF16), pltpu.VMEM((B_HEADS, LANE), BF16),
                        pltpu.VMEM((B_HEADS, 1), F32), pltpu.VMEM((B_HEADS, 1), F32),
                        pltpu.VMEM((B_HEADS, KV_RANK), F32),
                        pltpu.VMEM((KEYS_PER_STEP, KV_RANK), BF16), pltpu.VMEM((B_ROPE, KEYS_PER_STEP), BF16)],
    )
    return pl.pallas_call(
        _attn_sample_kernel,
        grid_spec=grid_spec,
        out_shape=jax.ShapeDtypeStruct((DEC_BATCH, 1, B_HEADS * B_DV), F32),
        compiler_params=_cparams(("parallel", "arbitrary")),
        name="attn_sample",
    )(page_table, q_s, ckvn_s, sg_s, wk_ext, e_rope, w_uv,
      *([cache_ckv] * PAGES_PER_STEP), *([cache_krope_t] * PAGES_PER_STEP))


def _merge_ffn_kernel(x_ref, hnm_ref, hnt_ref, obm_ref, obt_ref, ga_ref, gb_ref, wpa_ref, wpb_ref, wo_ref,
                      g1_ref, b1_ref, wup_ref, wdn_ref, g2_ref, b2_ref, o_ref):
    is_tail = pl.program_id(0) == N_TILES - 1
    hn = jnp.where(is_tail, hnt_ref[...], hnm_ref[...])
    ob = jnp.where(is_tail, obt_ref[...], obm_ref[...])
    y_a = _dot(hn, wpa_ref[...])
    y_b = _dot(ob, wpb_ref[...])
    mix = ga_ref[...].astype(F32) * y_a + gb_ref[...].astype(F32) * y_b
    y = _dot(mix.astype(BF16), wo_ref[...])
    x = _layer_norm(DN_ALPHA * x_ref[...] + y, g1_ref[...], b1_ref[...])
    o_ref[...] = _swiglu_postnorm(x, wup_ref, wdn_ref, g2_ref[...], b2_ref[...])


def _merge_ffn(x, hn, hn_tail, ob, ob_tail, ga, gb, layer, w_pa, w_pb, w_o, w_up, w_down, ln_idx, g, b):
    def row(n):
        return pl.BlockSpec((TM, n), lambda i: (i, 0))

    def main(n):
        return pl.BlockSpec((TM, n), lambda i: (jnp.minimum(i, N_REAL // TM - 1), 0))

    def tail(n):
        return pl.BlockSpec((TM, n), lambda i: (0, 0))

    return pl.pallas_call(
        _merge_ffn_kernel,
        grid=(N_TILES,),
        in_specs=[row(D_MODEL), main(A_VW), tail(A_VW), main(B_HEADS * B_DV), tail(B_HEADS * B_DV),
                  row(D_MODEL), row(D_MODEL),
                  _layer_spec(layer, (A_VW, D_MODEL)), _layer_spec(layer, (B_HEADS * B_DV, D_MODEL)),
                  _layer_spec(layer, (D_MODEL, D_MODEL)), _layer_spec(ln_idx, (1, D_MODEL)),
                  _layer_spec(ln_idx, (1, D_MODEL)),
                  _layer_spec(layer, (D_MODEL, 2 * D_FF)), _layer_spec(layer, (D_FF, D_MODEL)),
                  _layer_spec(ln_idx + 1, (1, D_MODEL)), _layer_spec(ln_idx + 1, (1, D_MODEL))],
        out_specs=row(D_MODEL),
        out_shape=jax.ShapeDtypeStruct((T_PAD, D_MODEL), F32),
        compiler_params=_cparams(("parallel",)),
        name="merge_ffn",
    )(x, hn, hn_tail, ob, ob_tail, ga, gb, w_pa, w_pb, w_o, g, b, w_up, w_down, g, b)


def _prep_layer(l, w_in, b_in, q_norm_g, kv_norm_g, w_uq, w_uk, w_uv):
    offs = np.cumsum((0,) + IN_SIZES)

    def grp(a, i):
        return a[..., offs[i]:offs[i + 1]]

    def arrange(a):
        lead = a.shape[:-1]
        z = lambda n: jnp.zeros(lead + (n,), F32)
        sg = jnp.concatenate([grp(a, 6), grp(a, 7), z(ROPE_LANE - 2 * A_HEADS), grp(a, 2),
                              z(LANE - ROPE_LANE - B_ROPE)], axis=-1)
        return jnp.concatenate([grp(a, 0), grp(a, 1), sg, grp(a, 3), grp(a, 4), grp(a, 5), grp(a, 8), grp(a, 9)],
                               axis=-1)

    pad_q = LANE - B_NOPE - B_ROPE
    wuq = jnp.pad(w_uq[l], ((0, 0), (0, 0), (0, pad_q))).reshape(Q_RANK, B_HEADS * LANE)
    wuk = jnp.pad(w_uk[l], ((0, 0), (0, 0), (0, LANE - B_NOPE)))
    place = np.zeros((LANE, B_HEADS, LANE), np.float32)
    for e in range(B_ROPE):
        place[ROPE_LANE + e, :, ROPE_LANE + e] = 1.0
    w_kf = jnp.concatenate([wuk.reshape(KV_RANK, B_HEADS * LANE),
                            jnp.asarray(place).reshape(LANE, B_HEADS * LANE)], axis=0)
    wk_ext = jnp.transpose(wuk, (1, 2, 0)).reshape(B_HEADS * LANE, KV_RANK)
    return {
        "w_in": arrange(w_in[l]).astype(BF16),
        "b_in": arrange(b_in[l])[None, :],
        "qg": q_norm_g[l][None, :],
        "kvg": kv_norm_g[l][None, :],
        "w_uq": wuq.astype(BF16),
        "w_kf": w_kf.astype(BF16),
        "w_uv": w_uv[l].reshape(KV_RANK, B_HEADS * B_DV).astype(BF16),
        "wk_ext": wk_ext.astype(BF16),
    }


def _rope_select():
    e = np.zeros((B_HEADS, LANE, LANE), np.float32)
    for r in range(B_ROPE):
        e[:, ROPE_LANE + r, ROPE_LANE + r] = 1.0
    return jnp.asarray(e.reshape(B_HEADS * LANE, LANE)).astype(BF16)


def _rope_tabs():
    pos_real = N_META + np.arange(SEQ)
    pos_tail = np.zeros((TM,), np.int64)
    pos_tail[:BATCH * N_META] = np.arange(BATCH * N_META) % N_META
    pos_tail[BATCH * N_META:TAIL_ROWS] = PAST_LEN
    pos = jnp.asarray(np.concatenate([pos_real, pos_tail]), dtype=jnp.int32)
    inv = ROPE_BASE ** (-jnp.arange(0, B_ROPE, 2, dtype=F32) / B_ROPE)
    ang = pos.astype(F32)[:, None] * inv[None, :]
    cos, sin = jnp.cos(ang), jnp.sin(ang)
    n = pos.shape[0]
    z = lambda w: jnp.zeros((n, w), F32)
    tail = LANE - ROPE_LANE - B_ROPE
    tc = jnp.concatenate([jnp.ones((n, ROPE_LANE), F32), cos, cos, z(tail)], axis=1)
    ts1 = jnp.concatenate([z(ROPE_LANE), -sin, z(HALF_ROPE), z(tail)], axis=1)
    ts2 = jnp.concatenate([z(ROPE_LANE), z(HALF_ROPE), sin, z(tail)], axis=1)
    return tc, ts1, ts2


def _tail_tile(meta_rows, samp_rows):
    width = meta_rows.shape[1]
    return jnp.concatenate([meta_rows, samp_rows.astype(meta_rows.dtype),
                            jnp.zeros((TM - TAIL_ROWS, width), meta_rows.dtype)], axis=0)


def kernel(x_prompt, x_sample, cache_ckv, cache_krope, page_table, state_C, state_n, state_m, state_conv, meta,
           w_in, b_in, conv_w, conv_b, mh_g, q_norm_g, kv_norm_g, w_uq, w_uk, w_uv, w_pa, w_pb, w_o,
           ffn1_up, ffn1_down, ffn2_up, ffn2_down, ln_g, ln_b):
    x = x_prompt.reshape(N_REAL, D_MODEL)
    x_tail = _tail_tile(jnp.tile(meta.astype(F32), (BATCH, 1)), x_sample.reshape(DEC_BATCH, D_MODEL))
    tabs = _rope_tabs()
    e_rope = _rope_select()
    pt_flat = page_table.reshape(-1)
    samp = slice(SAMP_OFF, SAMP_OFF + DEC_BATCH)
    metas = slice(META_OFF, SAMP_OFF)

    ffn1_up_b, ffn1_down_b = ffn1_up.astype(BF16), ffn1_down.astype(BF16)
    ffn2_up_b, ffn2_down_b = ffn2_up.astype(BF16), ffn2_down.astype(BF16)
    w_pa_b, w_pb_b, w_o_b = w_pa.astype(BF16), w_pb.astype(BF16), w_o.astype(BF16)
    ln_g3 = ln_g.reshape(DEPTH * 3, 1, D_MODEL)
    ln_b3 = ln_b.reshape(DEPTH * 3, 1, D_MODEL)
    state_m4 = state_m[:, :, None, :]
    cache_krope_t = jnp.swapaxes(cache_krope, 2, 3)
    conv_rows = (np.arange(BATCH)[:, None] * SEQ + (SEQ - (CONV_W - 1)) + np.arange(CONV_W - 1)[None, :]).reshape(-1)

    st = {k: [] for k in ("ckv_p", "kr_p", "c_p", "n_p", "m_p", "conv_p", "ckv_s", "kr_s", "c_s", "n_s", "m_s",
                          "conv_s")}
    for l in range(DEPTH):
        wp = _prep_layer(l, w_in, b_in, q_norm_g, kv_norm_g, w_uq, w_uk, w_uv)
        x = _ffn(x, l, ffn1_up_b, ffn1_down_b, 3 * l, ln_g3, ln_b3, x_tail=x_tail if l == 0 else None)
        q, k, vb, ckvn, sg, qk, vm, om, ga, gb = _inproj(x, wp, tabs)

        hn, hn_meta, c_p, n_p, m_p = mlstm_prompt_from_rows(qk, vm, om, sg[:, :2 * A_HEADS], conv_w[l], conv_b[l],
                                                            mh_g[l])
        hn_s, c_s, n_s, m_s = _mlstm_sample(
            l, qk[samp][:, None, :], state_conv, vm[samp].astype(F32)[:, None, :], om[samp][:, None, :],
            sg[samp][:, None, :], state_m4, state_C, state_n,
            conv_w[l], conv_b[l][None], mh_g[l][None])
        hn_tail = _tail_tile(hn_meta, hn_s.reshape(DEC_BATCH, A_VW))

        ob = _attn_prompt(q, k, vb)
        ob_meta = _attn_meta(q, k, vb)
        ob_s = _attn_sample(l, pt_flat, q[samp].astype(F32)[:, None, :], ckvn[samp][:, None, :],
                            sg[samp][:, None, :], wp["wk_ext"], e_rope, wp["w_uv"], cache_ckv, cache_krope_t)
        ob_tail = _tail_tile(ob_meta, ob_s.reshape(DEC_BATCH, B_HEADS * B_DV))

        x = _merge_ffn(x, hn, hn_tail, ob, ob_tail, ga, gb, l, w_pa_b, w_pb_b, w_o_b, ffn2_up_b, ffn2_down_b,
                       3 * l + 1, ln_g3, ln_b3)

        kr_all = sg[:, ROPE_LANE:ROPE_LANE + B_ROPE]
        st["ckv_p"].append(ckvn)
        st["kr_p"].append(kr_all)
        st["c_p"].append(c_p)
        st["n_p"].append(n_p)
        st["m_p"].append(m_p)
        st["conv_p"].append(jnp.take(qk, conv_rows, axis=0).reshape(BATCH, CONV_W - 1, 2 * A_QK))
        st["ckv_s"].append(ckvn[samp][:, None, :])
        st["kr_s"].append(kr_all[samp][:, None, :])
        st["c_s"].append(c_s)
        st["n_s"].append(n_s)
        st["m_s"].append(m_s[:, 0, :])
        st["conv_s"].append(jnp.concatenate([state_conv[l][:, 1:], qk[samp][:, None, :]], axis=1))

    y_prompt = x[:N_REAL].reshape(BATCH, SEQ, D_MODEL)
    y_sample = x[samp].reshape(DEC_BATCH, 1, D_MODEL)
    def seq_order_all(rows, width):
        meta_part = jnp.stack([a[metas] for a in rows]).reshape(DEPTH, BATCH, N_META, width)
        main_part = jnp.stack([a[:N_REAL] for a in rows]).reshape(DEPTH, BATCH, SEQ, width)
        return jnp.concatenate([meta_part, main_part], axis=2)

    ckv_p = seq_order_all(st.pop("ckv_p"), KV_RANK)
    kr_p = seq_order_all(st.pop("kr_p"), B_ROPE)
    s = {k: jnp.stack(v) for k, v in st.items()}
    s["ckv_p"], s["kr_p"] = ckv_p, kr_p
    return (y_prompt, y_sample, s["ckv_p"], s["kr_p"], s["c_p"], s["n_p"], s["m_p"], s["conv_p"],
            s["ckv_s"], s["kr_s"], s["c_s"], s["n_s"], s["m_s"], s["conv_s"])
```
